```python
import jax, jax.numpy as jnp
from jax import lax
import numpy as np

D_MODEL = 4096
BATCH = 2
SEQ = 8192
DEPTH = 2

HEAD_DIM = 128
N_GROUPS = 4
N_GROUP_HEADS = D_MODEL // HEAD_DIM // N_GROUPS
GROUP_WIDTH = N_GROUP_HEADS * HEAD_DIM
MIX_WIDTH = N_GROUPS * GROUP_WIDTH
RET_HEADS = N_GROUP_HEADS
RET_CHUNK = 128
MLA_HEADS = N_GROUP_HEADS
MLA_Q_LORA = GROUP_WIDTH
MLA_KV_LORA = 512
MLA_NOPE = 128
MLA_ROPE = 64
MLA_V = 128
FOX_HEADS = N_GROUP_HEADS
Q_BLOCK = 128
CONV_CH = GROUP_WIDTH
CONV_K = 31
D_FF = 11008
PLE_DIM = 256
ROPE_THETA = 10000.0
NORM_EPS = 1e-6
LN_EPS = 1e-5

SPLIT_SIZES = (GROUP_WIDTH, GROUP_WIDTH, GROUP_WIDTH, GROUP_WIDTH,
               MLA_Q_LORA, MLA_KV_LORA, MLA_ROPE,
               GROUP_WIDTH, GROUP_WIDTH, GROUP_WIDTH, FOX_HEADS,
               2 * CONV_CH)
SPLIT_POINTS = tuple(int(v) for v in np.cumsum(SPLIT_SIZES)[:-1])
N_IN = sum(SPLIT_SIZES)

kernel_name = "hybrid_parallel_ret_mla_fox_conv_macaron"

F32 = jnp.float32


def rms_norm(x, g):
    xf = x.astype(F32)
    y = xf * lax.rsqrt(jnp.mean(xf * xf, axis=-1, keepdims=True) + NORM_EPS)
    return (y * g.astype(F32)).astype(x.dtype)


def layer_norm(x, g, b):
    xf = x.astype(F32)
    mu = jnp.mean(xf, axis=-1, keepdims=True)
    var = jnp.mean(jnp.square(xf - mu), axis=-1, keepdims=True)
    y = (xf - mu) * lax.rsqrt(var + LN_EPS)
    return (y * g.astype(F32) + b.astype(F32)).astype(x.dtype)


def swiglu(x, w_gate, w_up, w_down):
    return (jax.nn.silu(x @ w_gate) * (x @ w_up)) @ w_down


def rope(x, positions):
    half = x.shape[-1] // 2
    inv_freq = ROPE_THETA ** (-jnp.arange(half, dtype=F32) / half)
    ang = positions.astype(F32)[..., None] * inv_freq
    cos = jnp.cos(ang)[:, :, None, :]
    sin = jnp.sin(ang)[:, :, None, :]
    x1 = x[..., :half].astype(F32)
    x2 = x[..., half:].astype(F32)
    out = jnp.concatenate([x1 * cos - x2 * sin, x2 * cos + x1 * sin], axis=-1)
    return out.astype(x.dtype)


def retention(q, k, v):
    B, H, S, d = q.shape
    n_chunks = S // RET_CHUNK
    log_gamma = jnp.log(1.0 - 2.0 ** (-5.0 - jnp.arange(H, dtype=F32)))
    idx = jnp.arange(RET_CHUNK, dtype=F32)
    lg = log_gamma[:, None]
    diff = idx[:, None] - idx[None, :]
    intra = jnp.where(diff >= 0, jnp.exp(lg[:, :, None] * jnp.maximum(diff, 0.0)), 0.0)
    q_dec = jnp.exp(lg * (idx + 1.0))[None, :, :, None]
    k_dec = jnp.exp(lg * (RET_CHUNK - 1.0 - idx))[None, :, :, None]
    chunk_dec = jnp.exp(log_gamma * RET_CHUNK)[None, :, None, None]

    def to_chunks(t):
        return t.astype(F32).reshape(B, H, n_chunks, RET_CHUNK, t.shape[-1]).transpose(2, 0, 1, 3, 4)

    def step(state, qkv):
        qc, kc, vc = qkv
        scores = jnp.einsum('bhid,bhjd->bhij', qc, kc) * intra
        inner = jnp.einsum('bhij,bhjd->bhid', scores, vc)
        cross = jnp.einsum('bhid,bhde->bhie', qc * q_dec, state)
        new_state = state * chunk_dec + jnp.einsum('bhjd,bhje->bhde', kc * k_dec, vc)
        return new_state, inner + cross

    state0 = jnp.zeros((B, H, d, v.shape[-1]), F32)
    _, out = lax.scan(step, state0, (to_chunks(q), to_chunks(k), to_chunks(v)))
    return out.transpose(1, 2, 0, 3, 4).reshape(B, H, S, v.shape[-1])


def causal_block_attention(q, k, v, scale, cum=None):
    B, H, S, dk = q.shape
    nb = S // Q_BLOCK
    q_blocks = q.reshape(B, H, nb, Q_BLOCK, dk).transpose(2, 0, 1, 3, 4)
    k_pos = jnp.arange(S)

    def one_block(i, q_blk, c_blk):
        s = jnp.einsum('bhqd,bhkd->bhqk', q_blk, k, preferred_element_type=F32) * scale
        if c_blk is not None:
            s = s + (c_blk[..., :, None] - cum[:, :, None, :])
        q_pos = i * Q_BLOCK + jnp.arange(Q_BLOCK)
        s = jnp.where(k_pos[None, :] <= q_pos[:, None], s, -jnp.inf)
        prob = jax.nn.softmax(s, axis=-1)
        return jnp.einsum('bhqk,bhkd->bhqd', prob.astype(v.dtype), v)

    if cum is None:
        out = lax.map(lambda a: one_block(a[0], a[1], None), (jnp.arange(nb), q_blocks))
    else:
        c_blocks = cum.reshape(B, H, nb, Q_BLOCK).transpose(2, 0, 1, 3)
        out = lax.map(lambda a: one_block(a[0], a[1], a[2]), (jnp.arange(nb), q_blocks, c_blocks))
    return out.transpose(1, 2, 0, 3, 4).reshape(B, H, S, v.shape[-1])


def token_mix(hn, positions, w_in, mla_q_norm, mla_w_uq, mla_kv_norm, mla_w_ukv,
              ret_gn_gain, ret_gn_bias, fox_forget_bias, conv_glu_bias, conv_dw, conv_dw_bias,
              conv_ln_gain, conv_ln_bias, conv_w_pw, conv_pw_bias, w_out):
    B, S, _ = hn.shape
    proj = hn @ w_in
    (r_q, r_k, r_v, r_g, m_cq, m_ckv, m_kr,
     f_q, f_k, f_v, f_f, c_in) = jnp.split(proj, SPLIT_POINTS, axis=-1)

    def heads(t, h):
        return t.reshape(B, S, h, -1)

    def to_bhsd(t):
        return t.transpose(0, 2, 1, 3)

    def merge(t):
        return t.transpose(0, 2, 1, 3).reshape(B, S, -1)

    rq = to_bhsd(rope(heads(r_q, RET_HEADS), positions))
    rk = to_bhsd(rope(heads(r_k, RET_HEADS), positions)) * HEAD_DIM ** -0.5
    rv = to_bhsd(heads(r_v, RET_HEADS))
    ro = retention(rq, rk, rv)
    mu = jnp.mean(ro, axis=-1, keepdims=True)
    var = jnp.mean(jnp.square(ro - mu), axis=-1, keepdims=True)
    ro = merge((ro - mu) * lax.rsqrt(var + LN_EPS))
    ro = (ro * ret_gn_gain.astype(F32) + ret_gn_bias.astype(F32)) * jax.nn.silu(r_g.astype(F32))
    y_ret = ro.astype(hn.dtype)

    mq = heads(rms_norm(m_cq, mla_q_norm) @ mla_w_uq, MLA_HEADS)
    mq = jnp.concatenate([mq[..., :MLA_NOPE], rope(mq[..., MLA_NOPE:], positions)], axis=-1)
    kv = heads(rms_norm(m_ckv, mla_kv_norm) @ mla_w_ukv, MLA_HEADS)
    k_rope = jnp.broadcast_to(rope(m_kr[:, :, None, :], positions), (B, S, MLA_HEADS, MLA_ROPE))
    mk = jnp.concatenate([kv[..., :MLA_NOPE], k_rope], axis=-1)
    mv = kv[..., MLA_NOPE:]
    y_mla = merge(causal_block_attention(to_bhsd(mq), to_bhsd(mk), to_bhsd(mv),
                                         (MLA_NOPE + MLA_ROPE) ** -0.5))

    log_f = jax.nn.log_sigmoid(f_f.astype(F32) + fox_forget_bias.astype(F32))
    cum = jnp.cumsum(log_f, axis=1).transpose(0, 2, 1)
    y_fox = merge(causal_block_attention(to_bhsd(heads(f_q, FOX_HEADS)), to_bhsd(heads(f_k, FOX_HEADS)),
                                         to_bhsd(heads(f_v, FOX_HEADS)), HEAD_DIM ** -0.5, cum))

    c_in = c_in + conv_glu_bias
    u = c_in[..., :CONV_CH] * jax.nn.sigmoid(c_in[..., CONV_CH:])
    y = lax.conv_general_dilated(u, conv_dw[:, None, :].astype(u.dtype), window_strides=(1,),
                                 padding=[(CONV_K - 1, 0)],
                                 dimension_numbers=('NWC', 'WIO', 'NWC'),
                                 feature_group_count=CONV_CH) + conv_dw_bias
    y = jax.nn.silu(layer_norm(y, conv_ln_gain, conv_ln_bias))
    y_conv = y @ conv_w_pw + conv_pw_bias

    return jnp.concatenate([y_ret, y_mla, y_fox, y_conv], axis=-1) @ w_out


def setup_inputs(seed: int = 0) -> dict:
    key = jax.random.key(seed)
    ks = iter(jax.random.split(key, 48))

    def w(shape, fan_in):
        return jax.random.normal(next(ks), (DEPTH,) + shape, F32) * fan_in ** -0.5

    def gain(n):
        return 1.0 + 0.02 * jax.random.normal(next(ks), (DEPTH, n), F32)

    def bias(n):
        return 0.02 * jax.random.normal(next(ks), (DEPTH, n), F32)

    x = jax.random.normal(next(ks), (BATCH, SEQ, D_MODEL), F32)
    p = jax.random.normal(next(ks), (DEPTH, BATCH, SEQ, PLE_DIM), F32)
    offset = jax.random.randint(next(ks), (BATCH, 1), 0, 4096, dtype=jnp.int32)
    positions = offset + jnp.arange(SEQ, dtype=jnp.int32)[None, :]
    return {
        "x": x,
        "p": p,
        "positions": positions,
        "ffn1_norm_pre": gain(D_MODEL),
        "ffn1_w_gate": w((D_MODEL, D_FF), D_MODEL),
        "ffn1_w_up": w((D_MODEL, D_FF), D_MODEL),
        "ffn1_w_down": w((D_FF, D_MODEL), D_FF),
        "ffn1_norm_post": gain(D_MODEL),
        "mix_norm_pre": gain(D_MODEL),
        "w_in": w((D_MODEL, N_IN), D_MODEL),
        "mla_q_norm": gain(MLA_Q_LORA),
        "mla_w_uq": w((MLA_Q_LORA, MLA_HEADS * (MLA_NOPE + MLA_ROPE)), MLA_Q_LORA),
        "mla_kv_norm": gain(MLA_KV_LORA),
        "mla_w_ukv": w((MLA_KV_LORA, MLA_HEADS * (MLA_NOPE + MLA_V)), MLA_KV_LORA),
        "ret_gn_gain": gain(GROUP_WIDTH),
        "ret_gn_bias": bias(GROUP_WIDTH),
        "fox_forget_bias": bias(FOX_HEADS),
        "conv_glu_bias": bias(2 * CONV_CH),
        "conv_dw": w((CONV_K, CONV_CH), CONV_K),
        "conv_dw_bias": bias(CONV_CH),
        "conv_ln_gain": gain(CONV_CH),
        "conv_ln_bias": bias(CONV_CH),
        "conv_w_pw": w((CONV_CH, CONV_CH), CONV_CH),
        "conv_pw_bias": bias(CONV_CH),
        "w_out": w((MIX_WIDTH, D_MODEL), MIX_WIDTH),
        "mix_norm_post": gain(D_MODEL),
        "ffn2_norm_pre": gain(D_MODEL),
        "ffn2_w_gate": w((D_MODEL, D_FF), D_MODEL),
        "ffn2_w_up": w((D_MODEL, D_FF), D_MODEL),
        "ffn2_w_down": w((D_FF, D_MODEL), D_FF),
        "ffn2_norm_post": gain(D_MODEL),
        "ple_norm_in": gain(D_MODEL),
        "ple_w_gate": w((D_MODEL, D_MODEL), D_MODEL),
        "ple_w_proj": w((PLE_DIM, D_MODEL), PLE_DIM),
        "ple_norm_post": gain(D_MODEL),
    }


def reference(x, p, positions,
              ffn1_norm_pre, ffn1_w_gate, ffn1_w_up, ffn1_w_down, ffn1_norm_post,
              mix_norm_pre, w_in, mla_q_norm, mla_w_uq, mla_kv_norm, mla_w_ukv,
              ret_gn_gain, ret_gn_bias, fox_forget_bias, conv_glu_bias, conv_dw, conv_dw_bias,
              conv_ln_gain, conv_ln_bias, conv_w_pw, conv_pw_bias, w_out, mix_norm_post,
              ffn2_norm_pre, ffn2_w_gate, ffn2_w_up, ffn2_w_down, ffn2_norm_post,
              ple_norm_in, ple_w_gate, ple_w_proj, ple_norm_post):
    h = x
    for i in range(DEPTH):
        hn = rms_norm(h, ffn1_norm_pre[i])
        h = h + 0.5 * rms_norm(swiglu(hn, ffn1_w_gate[i], ffn1_w_up[i], ffn1_w_down[i]), ffn1_norm_post[i])
        hn = rms_norm(h, mix_norm_pre[i])
        mixed = token_mix(hn, positions, w_in[i], mla_q_norm[i], mla_w_uq[i], mla_kv_norm[i], mla_w_ukv[i],
                          ret_gn_gain[i], ret_gn_bias[i], fox_forget_bias[i], conv_glu_bias[i], conv_dw[i],
                          conv_dw_bias[i], conv_ln_gain[i], conv_ln_bias[i], conv_w_pw[i], conv_pw_bias[i],
                          w_out[i])
        h = h + rms_norm(mixed, mix_norm_post[i])
        hn = rms_norm(h, ffn2_norm_pre[i])
        h = h + 0.5 * rms_norm(swiglu(hn, ffn2_w_gate[i], ffn2_w_up[i], ffn2_w_down[i]), ffn2_norm_post[i])
        gate = jax.nn.sigmoid(rms_norm(h, ple_norm_in[i]) @ ple_w_gate[i])
        h = h + rms_norm(gate * (p[i] @ ple_w_proj[i]), ple_norm_post[i])
    return h
```

```python
import functools

import numpy as np
import jax
import jax.numpy as jnp
from jax import lax
from jax.experimental import pallas as pl
from jax.experimental.pallas import tpu as pltpu

F32 = jnp.float32
BF16 = jnp.bfloat16

HEAD_DIM = 128
LANES = 128
N_GROUPS = 4
MLA_ROPE = 64
CONV_K = 31
CONV_HALO = 32
ROPE_THETA = 10000.0
NORM_EPS = 1e-6
LN_EPS = 1e-5
NEG_BIG = -1e30
VMEM_LIMIT = 56 * 1024 * 1024


def _cparams(n_axes):
    return pltpu.CompilerParams(dimension_semantics=("arbitrary",) * n_axes,
                                vmem_limit_bytes=VMEM_LIMIT)


def _tile(n, pref):
    if n <= pref:
        return n
    t = pref - pref % LANES
    while t >= LANES:
        if n % t == 0:
            return t
        t -= LANES
    raise ValueError(f"no lane-aligned tile for {n}")


def _rms(x, gain, eps=NORM_EPS):
    return x * lax.rsqrt(jnp.mean(x * x, axis=-1, keepdims=True) + eps) * gain


def _silu(x):
    return x * jax.nn.sigmoid(x)


ROW_CHUNK = 64
COL_CHUNK = 512


def _for_row_chunks(n_rows, fn):
    chunk = min(ROW_CHUNK, n_rows)

    def body(r, carry):
        fn(pl.ds(pl.multiple_of(r * chunk, chunk), chunk))
        return carry

    lax.fori_loop(0, n_rows // chunk, body, 0)


def _rms_to_bf16(src_ref, gain_ref, dst_ref):
    def chunk(rows):
        dst_ref[rows, :] = _rms(src_ref[rows, :], gain_ref[...]).astype(BF16)

    _for_row_chunks(src_ref.shape[0], chunk)


def _ffn_kernel(h_ref, gpre_ref, wg_ref, wu_ref, wd_ref, gpost_ref, o_ref, xn_ref):
    j = pl.program_id(1)

    @pl.when(j == 0)
    def _():
        _rms_to_bf16(h_ref, gpre_ref, xn_ref)
        o_ref[...] = jnp.zeros_like(o_ref)

    xn = xn_ref[...]
    g = jnp.dot(xn, wg_ref[...], preferred_element_type=F32)
    u = jnp.dot(xn, wu_ref[...], preferred_element_type=F32)
    a = (_silu(g) * u).astype(BF16)
    d_model = o_ref.shape[1]
    cc = min(COL_CHUNK, d_model)
    for n in range(d_model // cc):
        sl = slice(n * cc, (n + 1) * cc)
        o_ref[:, sl] += jnp.dot(a, wd_ref[:, sl], preferred_element_type=F32)

    @pl.when(j == pl.num_programs(1) - 1)
    def _():
        def chunk(rows):
            o_ref[rows, :] = h_ref[rows, :] + 0.5 * _rms(o_ref[rows, :], gpost_ref[...])

        _for_row_chunks(o_ref.shape[0], chunk)


def _ffn(h, g_pre, wg, wu, wd, g_post, *, tm, tf):
    T, D = h.shape
    Fp = wg.shape[1]
    return pl.pallas_call(
        _ffn_kernel,
        grid=(T // tm, Fp // tf),
        in_specs=[
            pl.BlockSpec((tm, D), lambda i, j: (i, 0)),
            pl.BlockSpec((1, D), lambda i, j: (0, 0)),
            pl.BlockSpec((D, tf), lambda i, j: (0, j)),
            pl.BlockSpec((D, tf), lambda i, j: (0, j)),
            pl.BlockSpec((tf, D), lambda i, j: (j, 0)),
            pl.BlockSpec((1, D), lambda i, j: (0, 0)),
        ],
        out_specs=pl.BlockSpec((tm, D), lambda i, j: (i, 0)),
        out_shape=jax.ShapeDtypeStruct((T, D), F32),
        scratch_shapes=[pltpu.VMEM((tm, D), BF16)],
        compiler_params=_cparams(2),
        name="ffn",
    )(h, g_pre, wg, wu, wd, g_post)


def _proj_kernel(h_ref, g_ref, w_ref, wf_ref, o_ref, of_ref, xn_ref):
    @pl.when(pl.program_id(1) == 0)
    def _():
        _rms_to_bf16(h_ref, g_ref, xn_ref)
        of_ref[...] = jnp.dot(xn_ref[...], wf_ref[...], preferred_element_type=F32)

    o_ref[...] = jnp.dot(xn_ref[...], w_ref[...], preferred_element_type=F32).astype(o_ref.dtype)


def _proj(h, g, w, wf, *, tm, tn):
    T, D = h.shape
    N = w.shape[1]
    return pl.pallas_call(
        _proj_kernel,
        grid=(T // tm, N // tn),
        in_specs=[
            pl.BlockSpec((tm, D), lambda i, j: (i, 0)),
            pl.BlockSpec((1, D), lambda i, j: (0, 0)),
            pl.BlockSpec((D, tn), lambda i, j: (0, j)),
            pl.BlockSpec((D, LANES), lambda i, j: (0, 0)),
        ],
        out_specs=[
            pl.BlockSpec((tm, tn), lambda i, j: (i, j)),
            pl.BlockSpec((tm, LANES), lambda i, j: (i, 0)),
        ],
        out_shape=[jax.ShapeDtypeStruct((T, N), BF16), jax.ShapeDtypeStruct((T, LANES), F32)],
        scratch_shapes=[pltpu.VMEM((tm, D), BF16)],
        compiler_params=_cparams(2),
        name="mix_in_proj",
    )(h, g, w, wf)


def _rope_kernel(pos_ref, fr_ref, fm_ref, sr_ref, mc_ref, ms_ref, cr_ref, snr_ref, cm_ref, snm_ref):
    pos = pos_ref[...]
    ang_r = pos * fr_ref[...]
    cr_ref[...] = jnp.cos(ang_r)
    snr_ref[...] = jnp.sin(ang_r) * sr_ref[...]
    ang_m = pos * fm_ref[...]
    cm_ref[...] = jnp.cos(ang_m) * mc_ref[...]
    snm_ref[...] = jnp.sin(ang_m) * ms_ref[...]


def _rope_tables(pos_col, *, tm):
    T = pos_col.shape[0]
    lane = np.arange(LANES)
    half_r = HEAD_DIM // 2
    half_m = MLA_ROPE // 2
    freq_r = (ROPE_THETA ** (-(lane % half_r).astype(np.float32) / half_r)).astype(np.float32)
    sign_r = np.where(lane < half_r, -1.0, 1.0).astype(np.float32)
    active = (lane % 64) < half_m
    freq_m = (ROPE_THETA ** (-(lane % 64 % half_m).astype(np.float32) / half_m)).astype(np.float32)
    mask_c = active.astype(np.float32)
    mask_s = np.where(active, np.where(lane < 64, -1.0, 1.0), 0.0).astype(np.float32)
    consts = [jnp.asarray(c[None, :]) for c in (freq_r, freq_m, sign_r, mask_c, mask_s)]
    row = pl.BlockSpec((1, LANES), lambda i: (0, 0))
    tab = pl.BlockSpec((tm, LANES), lambda i: (i, 0))
    return pl.pallas_call(
        _rope_kernel,
        grid=(T // tm,),
        in_specs=[pl.BlockSpec((tm, 1), lambda i: (i, 0)), row, row, row, row, row],
        out_specs=[tab, tab, tab, tab],
        out_shape=[jax.ShapeDtypeStruct((T, LANES), F32)] * 4,
        compiler_params=_cparams(1),
        name="rope_tables",
    )(pos_col, *consts)


def _rot(x, cos, sin):
    return x * cos + pltpu.roll(x, LANES // 2, 1) * sin


def _ret_kernel(q_ref, k_ref, v_ref, g_ref, cos_ref, sin_ref, dm_ref, qd_ref, kd_ref, cd_ref,
                gain_ref, bias_ref, o_ref, st_ref, *, n_heads):
    @pl.when(pl.program_id(1) == 0)
    def _():
        st_ref[...] = jnp.zeros_like(st_ref)

    cos = cos_ref[...]
    sin = sin_ref[...]
    for h in range(n_heads):
        sl = slice(h * HEAD_DIM, (h + 1) * HEAD_DIM)
        qr = _rot(q_ref[:, sl].astype(F32), cos, sin)
        kr = _rot(k_ref[:, sl].astype(F32), cos, sin)
        vb = v_ref[:, sl]
        sc = lax.dot_general(qr.astype(BF16), kr.astype(BF16), (((1,), (1,)), ((), ())),
                             preferred_element_type=F32) * dm_ref[h]
        inner = jnp.dot(sc.astype(BF16), vb, preferred_element_type=F32)
        st = st_ref[h]
        cross = jnp.dot((qr * qd_ref[h]).astype(BF16), st.astype(BF16), preferred_element_type=F32)
        kdt = (kr * kd_ref[h]).T.astype(BF16)
        st_ref[h] = st * cd_ref[h] + jnp.dot(kdt, vb, preferred_element_type=F32)
        o = inner + cross
        mu = jnp.mean(o, axis=-1, keepdims=True)
        oc = o - mu
        var = jnp.mean(oc * oc, axis=-1, keepdims=True)
        y = (oc * lax.rsqrt(var + LN_EPS)) * gain_ref[:, sl] + bias_ref[:, sl]
        o_ref[:, sl] = (y * _silu(g_ref[:, sl].astype(F32))).astype(o_ref.dtype)


def _retention(proj, cos_r, sin_r, gain, bias, *, B, S, GW, C):
    T = B * S
    H = GW // HEAD_DIM
    nC = S // C
    idx = np.arange(C, dtype=np.float64)
    log_gamma = np.log(1.0 - 2.0 ** (-5.0 - np.arange(H, dtype=np.float64)))[:, None, None]
    diff = idx[:, None] - idx[None, :]
    dmat = np.where(diff >= 0, np.exp(log_gamma * np.maximum(diff, 0.0)), 0.0)
    qdec = np.broadcast_to(np.exp(log_gamma * (idx + 1.0)[None, :, None]), (H, C, LANES))
    kdec = np.broadcast_to(np.exp(log_gamma * (C - 1.0 - idx)[None, :, None]), (H, C, LANES))
    cdec = np.broadcast_to(np.exp(log_gamma * C), (H, 1, LANES))
    consts = [jnp.asarray(np.ascontiguousarray(c), F32) for c in (dmat, qdec, kdec, cdec)]

    def col(c):
        return pl.BlockSpec((C, GW), lambda b, i, c=c: (b * nC + i, c))

    def const(shape):
        return pl.BlockSpec(shape, lambda b, i: (0,) * len(shape))

    tab = pl.BlockSpec((C, LANES), lambda b, i: (b * nC + i, 0))
    return pl.pallas_call(
        functools.partial(_ret_kernel, n_heads=H),
        grid=(B, nC),
        in_specs=[col(0), col(1), col(2), col(3), tab, tab,
                  const((H, C, C)), const((H, C, LANES)), const((H, C, LANES)), const((H, 1, LANES)),
                  const((1, GW)), const((1, GW))],
        out_specs=pl.BlockSpec((C, GW), lambda b, i: (b * nC + i, 0)),
        out_shape=jax.ShapeDtypeStruct((T, GW), BF16),
        scratch_shapes=[pltpu.VMEM((H, HEAD_DIM, HEAD_DIM), F32)],
        compiler_params=_cparams(2),
        name="retention",
    )(proj, proj, proj, proj, cos_r, sin_r, *consts, gain, bias)


def _mla_prep_kernel(cq_ref, ckv_ref, kr_ref, gq_ref, gkv_ref, wq_ref, wkv_ref, cos_ref, sin_ref,
                     qn_ref, qr_ref, kn_ref, v_ref, kro_ref, *, gw):
    cos = cos_ref[...]
    sin = sin_ref[...]
    cqn = _rms(cq_ref[...].astype(F32), gq_ref[...]).astype(BF16)
    q = jnp.dot(cqn, wq_ref[...], preferred_element_type=F32)
    qn_ref[...] = q[:, :gw].astype(BF16)
    for h in range(gw // HEAD_DIM):
        sl = slice(h * HEAD_DIM, (h + 1) * HEAD_DIM)
        qr_ref[:, sl] = _rot(q[:, gw + h * HEAD_DIM:gw + (h + 1) * HEAD_DIM], cos, sin).astype(BF16)
    ckn = _rms(ckv_ref[...].astype(F32), gkv_ref[...]).astype(BF16)
    kv = jnp.dot(ckn, wkv_ref[...], preferred_element_type=F32)
    kn_ref[...] = kv[:, :gw].astype(BF16)
    v_ref[...] = kv[:, gw:].astype(BF16)
    kro_ref[...] = _rot(kr_ref[...].astype(F32), cos, sin).astype(BF16)


def _mla_prep(proj, gq, gkv, wq, wkv, cos_m, sin_m, *, GW, KVL, tm, off_cq, off_ckv, off_kr):
    T = proj.shape[0]
    row = lambda n: pl.BlockSpec((1, n), lambda i: (0, 0))
    blk = lambda n: pl.BlockSpec((tm, n), lambda i: (i, 0))
    return pl.pallas_call(
        functools.partial(_mla_prep_kernel, gw=GW),
        grid=(T // tm,),
        in_specs=[
            pl.BlockSpec((tm, GW), lambda i: (i, off_cq // GW)),
            pl.BlockSpec((tm, KVL), lambda i: (i, off_ckv // KVL)),
            pl.BlockSpec((tm, LANES), lambda i: (i, off_kr // LANES)),
            row(GW), row(KVL),
            pl.BlockSpec((GW, 2 * GW), lambda i: (0, 0)),
            pl.BlockSpec((KVL, 2 * GW), lambda i: (0, 0)),
            blk(LANES), blk(LANES),
        ],
        out_specs=[blk(GW), blk(GW), blk(GW), blk(GW), blk(LANES)],
        out_shape=[jax.ShapeDtypeStruct((T, GW), BF16)] * 4 + [jax.ShapeDtypeStruct((T, LANES), BF16)],
        compiler_params=_cparams(1),
        name="mla_prep",
    )(proj, proj, proj, gq, gkv, wq, wkv, cos_m, sin_m)


def _split3(x):
    hi = x.astype(BF16)
    r1 = x - hi.astype(F32)
    mid = r1.astype(BF16)
    lo = (r1 - mid.astype(F32)).astype(BF16)
    return hi, mid, lo


def _fox_prep_kernel(lg_ref, fb_ref, tri_ref, selq_ref, selk_ref, oneq_ref, onek_ref,
                     qe_ref, ke_ref, carry_ref, *, tiles_per_seq):
    @pl.when(pl.program_id(0) % tiles_per_seq == 0)
    def _():
        carry_ref[...] = jnp.zeros_like(carry_ref)

    x = lg_ref[...] + fb_ref[...]
    logf = jnp.minimum(x, 0.0) - jnp.log(1.0 + jnp.exp(-jnp.abs(x)))
    tri = tri_ref[...]
    cum = carry_ref[...]
    for part in _split3(logf):
        cum = cum + jnp.dot(tri, part, preferred_element_type=F32)
    carry_ref[...] = cum[cum.shape[0] - 1:, :]
    parts = jnp.concatenate(_split3(cum), axis=-1)
    qe_ref[...] = (jnp.dot(parts, selq_ref[...], preferred_element_type=F32) + oneq_ref[...]).astype(BF16)
    ke_ref[...] = (jnp.dot(parts, selk_ref[...], preferred_element_type=F32) + onek_ref[...]).astype(BF16)


def _fox_prep(logits, fbias, *, S, GW, tm):
    T = logits.shape[0]
    H = GW // HEAD_DIM
    tri = np.tril(np.ones((tm, tm), np.float32))
    selq = np.zeros((3 * LANES, GW), np.float32)
    selk = np.zeros((3 * LANES, GW), np.float32)
    oneq = np.zeros((1, GW), np.float32)
    onek = np.zeros((1, GW), np.float32)
    for h in range(H):
        for part in range(3):
            selq[part * LANES + h, h * HEAD_DIM + part] = 1.0
            selk[part * LANES + h, h * HEAD_DIM + 3 + part] = -1.0
            oneq[0, h * HEAD_DIM + 3 + part] = 1.0
            onek[0, h * HEAD_DIM + part] = 1.0
    const = lambda a: pl.BlockSpec(a.shape, lambda i: (0, 0))
    consts = [jnp.asarray(tri, BF16), jnp.asarray(selq, BF16), jnp.asarray(selk, BF16),
              jnp.asarray(oneq), jnp.asarray(onek)]
    return pl.pallas_call(
        functools.partial(_fox_prep_kernel, tiles_per_seq=S // tm),
        grid=(T // tm,),
        in_specs=[pl.BlockSpec((tm, LANES), lambda i: (i, 0)), pl.BlockSpec((1, LANES), lambda i: (0, 0))]
                 + [const(c) for c in consts],
        out_specs=[pl.BlockSpec((tm, GW), lambda i: (i, 0))] * 2,
        out_shape=[jax.ShapeDtypeStruct((T, GW), BF16)] * 2,
        scratch_shapes=[pltpu.VMEM((1, LANES), F32)],
        compiler_params=_cparams(1),
        name="fox_prep",
    )(logits, fbias, *consts)


def _flash_kernel(qm_ref, qe_ref, km_ref, ke_ref, v_ref, o_ref, m_ref, l_ref, acc_ref, *, t):
    i = pl.program_id(2)
    q = jnp.concatenate([qm_ref[...], qe_ref[...]], axis=-1)
    m_ref[...] = jnp.full_like(m_ref, NEG_BIG)
    l_ref[...] = jnp.zeros_like(l_ref)
    acc_ref[...] = jnp.zeros_like(acc_ref)

    def step(j, masked):
        rows = pl.ds(pl.multiple_of(j * t, t), t)
        k = jnp.concatenate([km_ref[rows, :], ke_ref[rows, :]], axis=-1)
        s = lax.dot_general(q, k, (((1,), (1,)), ((), ())), preferred_element_type=F32)
        if masked:
            r = lax.broadcasted_iota(jnp.int32, (t, t), 0)
            c = lax.broadcasted_iota(jnp.int32, (t, t), 1)
            s = jnp.where(c <= r, s, NEG_BIG)
        m_prev = m_ref[...]
        m_new = jnp.maximum(m_prev, jnp.max(s, axis=-1, keepdims=True))
        alpha = jnp.exp(m_prev - m_new)
        p = jnp.exp(s - m_new)
        l_ref[...] = alpha * l_ref[...] + jnp.sum(p, axis=-1, keepdims=True)
        acc_ref[...] = alpha * acc_ref[...] + jnp.dot(p.astype(BF16), v_ref[rows, :],
                                                      preferred_element_type=F32)
        m_ref[...] = m_new

    def body(j, carry):
        step(j, False)
        return carry

    lax.fori_loop(0, i, body, 0)
    step(i, True)
    o_ref[...] = (acc_ref[...] / l_ref[...]).astype(o_ref.dtype)


def _flash(qm, qe, km, ke, v, *, B, S, H, t, qm_off=0, km_off=0, v_off=0, ke_shared=False):
    T = B * S
    nq = S // t
    qspec = lambda off: pl.BlockSpec((t, HEAD_DIM), lambda b, h, i: (b * nq + i, off + h))
    kspec = lambda off: pl.BlockSpec((S, HEAD_DIM), lambda b, h, i: (b, off + h))
    ke_spec = pl.BlockSpec((S, HEAD_DIM), (lambda b, h, i: (b, 0)) if ke_shared else (lambda b, h, i: (b, h)))
    return pl.pallas_call(
        functools.partial(_flash_kernel, t=t),
        grid=(B, H, nq),
        in_specs=[qspec(qm_off), qspec(0), kspec(km_off), ke_spec, kspec(v_off)],
        out_specs=pl.BlockSpec((t, HEAD_DIM), lambda b, h, i: (b * nq + i, h)),
        out_shape=jax.ShapeDtypeStruct((T, H * HEAD_DIM), BF16),
        scratch_shapes=[pltpu.VMEM((t, 1), F32), pltpu.VMEM((t, 1), F32), pltpu.VMEM((t, HEAD_DIM), F32)],
        compiler_params=_cparams(3),
        name="causal_attention",
    )(qm, qe, km, ke, v)


def _dwconv_kernel(a_ref, b_ref, ba_ref, bb_ref, w_ref, wb_ref, o_ref, u_ref, halo_ref, *, tm, tiles_per_seq):
    c = pl.program_id(1)

    @pl.when(pl.program_id(0) % tiles_per_seq == 0)
    def _():
        halo_ref[c] = jnp.zeros((CONV_HALO, LANES), F32)

    u = (a_ref[...].astype(F32) + ba_ref[...]) * jax.nn.sigmoid(b_ref[...].astype(F32) + bb_ref[...])
    u_ref[:CONV_HALO, :] = halo_ref[c]
    u_ref[CONV_HALO:, :] = u
    halo_ref[c] = u[tm - CONV_HALO:, :]
    acc = jnp.zeros((tm, LANES), F32) + wb_ref[...]
    for j in range(CONV_K):
        start = CONV_HALO - (CONV_K - 1) + j
        acc = acc + u_ref[start:start + tm, :] * w_ref[j:j + 1, :]
    o_ref[...] = acc


def _dwconv(proj, glu_bias_a, glu_bias_b, w_dw, b_dw, *, S, GW, tm, off_a, off_b):
    T = proj.shape[0]
    nc = GW // LANES
    row = pl.BlockSpec((1, LANES), lambda i, c: (0, c))
    return pl.pallas_call(
        functools.partial(_dwconv_kernel, tm=tm, tiles_per_seq=S // tm),
        grid=(T // tm, nc),
        in_specs=[
            pl.BlockSpec((tm, LANES), lambda i, c: (i, off_a // LANES + c)),
            pl.BlockSpec((tm, LANES), lambda i, c: (i, off_b // LANES + c)),
            row, row,
            pl.BlockSpec((CONV_HALO, LANES), lambda i, c: (0, c)),
            row,
        ],
        out_specs=pl.BlockSpec((tm, LANES), lambda i, c: (i, c)),
        out_shape=jax.ShapeDtypeStruct((T, GW), F32),
        scratch_shapes=[pltpu.VMEM((tm + CONV_HALO, LANES), F32), pltpu.VMEM((nc, CONV_HALO, LANES), F32)],
        compiler_params=_cparams(2),
        name="glu_dwconv",
    )(proj, proj, glu_bias_a, glu_bias_b, w_dw, b_dw)


def _conv_pw_kernel(y_ref, g_ref, b_ref, w_ref, pb_ref, o_ref):
    y = y_ref[...]
    mu = jnp.mean(y, axis=-1, keepdims=True)
    yc = y - mu
    var = jnp.mean(yc * yc, axis=-1, keepdims=True)
    z = _silu(yc * lax.rsqrt(var + LN_EPS) * g_ref[...] + b_ref[...]).astype(BF16)
    o_ref[...] = (jnp.dot(z, w_ref[...], preferred_element_type=F32) + pb_ref[...]).astype(o_ref.dtype)


def _conv_pw(y, gain, bias, w_pw, pw_bias, *, tm):
    T, GW = y.shape
    row = pl.BlockSpec((1, GW), lambda i: (0, 0))
    blk = pl.BlockSpec((tm, GW), lambda i: (i, 0))
    return pl.pallas_call(
        _conv_pw_kernel,
        grid=(T // tm,),
        in_specs=[blk, row, row, pl.BlockSpec((GW, GW), lambda i: (0, 0)), row],
        out_specs=blk,
        out_shape=jax.ShapeDtypeStruct((T, GW), BF16),
        compiler_params=_cparams(1),
        name="conv_ln_pointwise",
    )(y, gain, bias, w_pw, pw_bias)


def _park_tile(o_ref, ss_ref, z, j, tn):
    o_ref[:, pl.ds(pl.multiple_of(j * tn, tn), tn)] = z
    ss_ref[...] += jnp.sum(z * z, axis=-1, keepdims=True)


def _finish_rows(h_ref, gain_ref, o_ref, ss_ref):
    d_model = o_ref.shape[1]

    def chunk(rows):
        r = lax.rsqrt(ss_ref[rows, :] * (1.0 / d_model) + NORM_EPS)
        o_ref[rows, :] = h_ref[rows, :] + o_ref[rows, :] * r * gain_ref[...]

    _for_row_chunks(o_ref.shape[0], chunk)


def _out_proj_kernel(y0_ref, y1_ref, y2_ref, y3_ref, w_ref, h_ref, g_ref, o_ref, ss_ref,
                     *, gw, tn, n_tiles):
    j = pl.program_id(1)

    @pl.when(j == 0)
    def _():
        ss_ref[...] = jnp.zeros_like(ss_ref)

    z = jnp.dot(y0_ref[...], w_ref[0:gw, :], preferred_element_type=F32)
    z += jnp.dot(y1_ref[...], w_ref[gw:2 * gw, :], preferred_element_type=F32)
    z += jnp.dot(y2_ref[...], w_ref[2 * gw:3 * gw, :], preferred_element_type=F32)
    z += jnp.dot(y3_ref[...], w_ref[3 * gw:4 * gw, :], preferred_element_type=F32)
    _park_tile(o_ref, ss_ref, z, j, tn)

    @pl.when(j == n_tiles - 1)
    def _():
        _finish_rows(h_ref, g_ref, o_ref, ss_ref)


def _out_proj(ys, w, h, g, *, tm, tn):
    T, D = h.shape
    GW = ys[0].shape[1]
    n_tiles = D // tn
    yspec = pl.BlockSpec((tm, GW), lambda i, j: (i, 0))
    return pl.pallas_call(
        functools.partial(_out_proj_kernel, gw=GW, tn=tn, n_tiles=n_tiles),
        grid=(T // tm, n_tiles),
        in_specs=[yspec, yspec, yspec, yspec,
                  pl.BlockSpec((N_GROUPS * GW, tn), lambda i, j: (0, j)),
                  pl.BlockSpec((tm, D), lambda i, j: (i, 0)),
                  pl.BlockSpec((1, D), lambda i, j: (0, 0))],
        out_specs=pl.BlockSpec((tm, D), lambda i, j: (i, 0)),
        out_shape=jax.ShapeDtypeStruct((T, D), F32),
        scratch_shapes=[pltpu.VMEM((tm, 1), F32)],
        compiler_params=_cparams(2),
        name="mix_out_proj",
    )(*ys, w, h, g)


def _ple_kernel(h_ref, gin_ref, wg_ref, p_ref, wp_ref, gpost_ref, o_ref, xn_ref, ss_ref, *, tn, n_tiles):
    j = pl.program_id(1)

    @pl.when(j == 0)
    def _():
        _rms_to_bf16(h_ref, gin_ref, xn_ref)
        ss_ref[...] = jnp.zeros_like(ss_ref)

    gate = jax.nn.sigmoid(jnp.dot(xn_ref[...], wg_ref[...], preferred_element_type=F32))
    z = gate * jnp.dot(p_ref[...], wp_ref[...], preferred_element_type=F32)
    _park_tile(o_ref, ss_ref, z, j, tn)

    @pl.when(j == n_tiles - 1)
    def _():
        _finish_rows(h_ref, gpost_ref, o_ref, ss_ref)


def _ple(h, g_in, wg, p, wp, g_post, *, tm, tn):
    T, D = h.shape
    P = p.shape[1]
    n_tiles = D // tn
    return pl.pallas_call(
        functools.partial(_ple_kernel, tn=tn, n_tiles=n_tiles),
        grid=(T // tm, n_tiles),
        in_specs=[pl.BlockSpec((tm, D), lambda i, j: (i, 0)),
                  pl.BlockSpec((1, D), lambda i, j: (0, 0)),
                  pl.BlockSpec((D, tn), lambda i, j: (0, j)),
                  pl.BlockSpec((tm, P), lambda i, j: (i, 0)),
                  pl.BlockSpec((P, tn), lambda i, j: (0, j)),
                  pl.BlockSpec((1, D), lambda i, j: (0, 0))],
        out_specs=pl.BlockSpec((tm, D), lambda i, j: (i, 0)),
        out_shape=jax.ShapeDtypeStruct((T, D), F32),
        scratch_shapes=[pltpu.VMEM((tm, D), BF16), pltpu.VMEM((tm, 1), F32)],
        compiler_params=_cparams(2),
        name="ple",
    )(h, g_in, wg, p, wp, g_post)


def _rope_lane_layout(w, half):
    z = jnp.zeros((w.shape[0], LANES // 2 - half), w.dtype)
    return jnp.concatenate([w[:, :half], z, w[:, half:], z], axis=1)


def _pad_cols(w, n):
    return jnp.pad(w, ((0, 0), (0, n - w.shape[1])))


def _plan(D, F, S, T):
    return dict(
        tm=_tile(T if T < S else S, 512),
        tf=_tile(F, 256),
        tn=_tile(D, 512),
        t_attn=_tile(S, 512),
        c_ret=_tile(S, 256),
    )


def kernel(x, p, positions, ffn1_norm_pre, ffn1_w_gate, ffn1_w_up, ffn1_w_down, ffn1_norm_post, mix_norm_pre, w_in, mla_q_norm, mla_w_uq, mla_kv_norm, mla_w_ukv, ret_gn_gain, ret_gn_bias, fox_forget_bias, conv_glu_bias, conv_dw, conv_dw_bias, conv_ln_gain, conv_ln_bias, conv_w_pw, conv_pw_bias, w_out, mix_norm_post, ffn2_norm_pre, ffn2_w_gate, ffn2_w_up, ffn2_w_down, ffn2_norm_post, ple_norm_in, ple_w_gate, ple_w_proj, ple_norm_post):
    B, S, D = x.shape
    depth = p.shape[0]
    T = B * S
    F = ffn1_w_gate.shape[2]
    GW = D // N_GROUPS
    H = GW // HEAD_DIM
    KVL = mla_kv_norm.shape[1]
    NOPE = HEAD_DIM
    plan = _plan(D, F, S, T)
    tm, tf, tn, t_attn, c_ret = plan["tm"], plan["tf"], plan["tn"], plan["t_attn"], plan["c_ret"]
    row = lambda v: v.reshape(1, -1).astype(F32)

    h = x.reshape(T, D)
    pos_col = positions.reshape(T, 1).astype(F32)
    cos_r, sin_r, cos_m, sin_m = _rope_tables(pos_col, tm=tm)

    off_ret, off_fox, off_conv = 0, 4 * GW, 7 * GW
    off_cq, off_ckv, off_kr = 9 * GW, 10 * GW, 10 * GW + KVL
    n_main = off_kr + LANES
    n_main_pad = -(-n_main // tn) * tn
    sp = np.cumsum([GW] * 4 + [GW, KVL, MLA_ROPE] + [GW] * 3 + [H, 2 * GW])
    ret_scale = HEAD_DIM ** -0.5
    fox_scale = HEAD_DIM ** -0.5
    mla_scale = (NOPE + MLA_ROPE) ** -0.5

    for i in range(depth):
        wi = w_in[i]
        r_q, r_k, r_v, r_g = wi[:, :sp[0]], wi[:, sp[0]:sp[1]], wi[:, sp[1]:sp[2]], wi[:, sp[2]:sp[3]]
        m_cq, m_ckv, m_kr = wi[:, sp[3]:sp[4]], wi[:, sp[4]:sp[5]], wi[:, sp[5]:sp[6]]
        f_q, f_k, f_v, f_f = wi[:, sp[6]:sp[7]], wi[:, sp[7]:sp[8]], wi[:, sp[8]:sp[9]], wi[:, sp[9]:sp[10]]
        c_in = wi[:, sp[10]:sp[11]]
        w_main = jnp.concatenate(
            [r_q, r_k * ret_scale, r_v, r_g, f_q * fox_scale, f_k, f_v, c_in, m_cq, m_ckv,
             _rope_lane_layout(m_kr, MLA_ROPE // 2)], axis=1)
        w_main = _pad_cols(w_main, n_main_pad).astype(BF16)
        w_f = _pad_cols(f_f, LANES).astype(BF16)
        uq = mla_w_uq[i].reshape(GW, H, NOPE + MLA_ROPE) * mla_scale
        uq_nope = uq[:, :, :NOPE].reshape(GW, H * NOPE)
        uq_rope = jnp.concatenate(
            [_rope_lane_layout(uq[:, hh, NOPE:], MLA_ROPE // 2) for hh in range(H)], axis=1)
        w_uq = jnp.concatenate([uq_nope, uq_rope], axis=1).astype(BF16)
        ukv = mla_w_ukv[i].reshape(KVL, H, 2 * HEAD_DIM)
        w_ukv = jnp.concatenate([ukv[:, :, :NOPE].reshape(KVL, GW), ukv[:, :, NOPE:].reshape(KVL, GW)],
                                axis=1).astype(BF16)
        Fp = -(-F // tf) * tf
        ffn_w = []
        for wg_, wu_, wd_ in ((ffn1_w_gate, ffn1_w_up, ffn1_w_down), (ffn2_w_gate, ffn2_w_up, ffn2_w_down)):
            ffn_w.append((_pad_cols(wg_[i], Fp).astype(BF16), _pad_cols(wu_[i], Fp).astype(BF16),
                          jnp.pad(wd_[i], ((0, Fp - F), (0, 0))).astype(BF16)))
        dw = jnp.pad(conv_dw[i], ((0, CONV_HALO - CONV_K), (0, 0)))

        h = _ffn(h, row(ffn1_norm_pre[i]), *ffn_w[0], row(ffn1_norm_post[i]), tm=tm, tf=tf)

        proj, logits = _proj(h, row(mix_norm_pre[i]), w_main, w_f, tm=tm, tn=tn)
        y_ret = _retention(proj, cos_r, sin_r, row(ret_gn_gain[i]), row(ret_gn_bias[i]), B=B, S=S, GW=GW, C=c_ret)
        qn, qr, kn, mv, kro = _mla_prep(proj, row(mla_q_norm[i]), row(mla_kv_norm[i]), w_uq, w_ukv, cos_m, sin_m,
                                        GW=GW, KVL=KVL, tm=tm, off_cq=off_cq, off_ckv=off_ckv, off_kr=off_kr)
        y_mla = _flash(qn, qr, kn, kro, mv, B=B, S=S, H=H, t=t_attn, ke_shared=True)
        fbias = jnp.pad(fox_forget_bias[i], (0, LANES - H)).reshape(1, LANES)
        qe, ke = _fox_prep(logits, fbias, S=S, GW=GW, tm=tm)
        hb = off_fox // HEAD_DIM
        y_fox = _flash(proj, qe, proj, ke, proj, B=B, S=S, H=H, t=t_attn,
                       qm_off=hb, km_off=hb + H, v_off=hb + 2 * H)
        gb = conv_glu_bias[i]
        y_dw = _dwconv(proj, row(gb[:GW]), row(gb[GW:]), dw, row(conv_dw_bias[i]),
                       S=S, GW=GW, tm=tm, off_a=off_conv, off_b=off_conv + GW)
        y_conv = _conv_pw(y_dw, row(conv_ln_gain[i]), row(conv_ln_bias[i]), conv_w_pw[i].astype(BF16),
                          row(conv_pw_bias[i]), tm=tm)
        h = _out_proj([y_ret, y_mla, y_fox, y_conv], w_out[i].astype(BF16), h, row(mix_norm_post[i]), tm=tm, tn=tn)

        h = _ffn(h, row(ffn2_norm_pre[i]), *ffn_w[1], row(ffn2_norm_post[i]), tm=tm, tf=tf)

        h = _ple(h, row(ple_norm_in[i]), ple_w_gate[i].astype(BF16), p[i].reshape(T, -1).astype(BF16),
                 ple_w_proj[i].astype(BF16), row(ple_norm_post[i]), tm=tm, tn=tn)

    return h.reshape(B, S, D)
```

```python
import functools

import numpy as np
import jax
import jax.numpy as jnp
from jax import lax
from jax.experimental import pallas as pl
from jax.experimental.pallas import tpu as pltpu

F32 = jnp.float32
BF16 = jnp.bfloat16

HEAD_DIM = 128
LANES = 128
N_GROUPS = 4
MLA_ROPE = 64
CONV_K = 31
CONV_HALO = 32
ROPE_THETA = 10000.0
NORM_EPS = 1e-6
LN_EPS = 1e-5
NEG_BIG = -1e30
VMEM_LIMIT = 56 * 1024 * 1024


def _cparams(n_axes):
    return pltpu.CompilerParams(dimension_semantics=("arbitrary",) * n_axes,
                                vmem_limit_bytes=VMEM_LIMIT)


def _tile(n, pref):
    if n <= pref:
        return n
    t = pref - pref % LANES
    while t >= LANES:
        if n % t == 0:
            return t
        t -= LANES
    raise ValueError(f"no lane-aligned tile for {n}")


def _rms(x, gain, eps=NORM_EPS):
    return x * lax.rsqrt(jnp.mean(x * x, axis=-1, keepdims=True) + eps) * gain


def _silu(x):
    return x * jax.nn.sigmoid(x)


ROW_CHUNK = 64
COL_CHUNK = 512


def _for_row_chunks(n_rows, fn):
    chunk = min(ROW_CHUNK, n_rows)

    def body(r, carry):
        fn(pl.ds(pl.multiple_of(r * chunk, chunk), chunk))
        return carry

    lax.fori_loop(0, n_rows // chunk, body, 0)


def _rms_to_bf16(src_ref, gain_ref, dst_ref):
    def chunk(rows):
        dst_ref[rows, :] = _rms(src_ref[rows, :], gain_ref[...]).astype(BF16)

    _for_row_chunks(src_ref.shape[0], chunk)


CAST_BLOCK_BYTES = 8 * 1024 * 1024
BF16_SUBLANES = 16


def _cast_kernel(x_ref, o_ref):
    o_ref[...] = x_ref[...].astype(BF16)


def _to_bf16(w_stack, layer):
    _, R, C = w_stack.shape
    rows = R
    for cand in range(BF16_SUBLANES, R + 1, BF16_SUBLANES):
        if R % cand == 0 and cand * C * 4 <= CAST_BLOCK_BYTES:
            rows = cand
    if rows == R and R * C * 4 > CAST_BLOCK_BYTES:
        return w_stack[layer].astype(BF16)
    return pl.pallas_call(
        _cast_kernel,
        grid=(R // rows,),
        in_specs=[pl.BlockSpec((None, rows, C), lambda r: (layer, r, 0))],
        out_specs=pl.BlockSpec((rows, C), lambda r: (r, 0)),
        out_shape=jax.ShapeDtypeStruct((R, C), BF16),
        compiler_params=_cparams(1),
        name="cast_bf16",
    )(w_stack)


def _ffn_kernel(h_ref, gpre_ref, wg_ref, wu_ref, wd_ref, gpost_ref, o_ref, xn_ref):
    j = pl.program_id(1)

    @pl.when(j == 0)
    def _():
        _rms_to_bf16(h_ref, gpre_ref, xn_ref)
        o_ref[...] = jnp.zeros_like(o_ref)

    xn = xn_ref[...]
    g = jnp.dot(xn, wg_ref[...], preferred_element_type=F32)
    u = jnp.dot(xn, wu_ref[...], preferred_element_type=F32)
    a = (_silu(g) * u).astype(BF16)
    d_model = o_ref.shape[1]
    cc = min(COL_CHUNK, d_model)
    for n in range(d_model // cc):
        sl = slice(n * cc, (n + 1) * cc)
        o_ref[:, sl] += jnp.dot(a, wd_ref[:, sl], preferred_element_type=F32)

    @pl.when(j == pl.num_programs(1) - 1)
    def _():
        def chunk(rows):
            o_ref[rows, :] = h_ref[rows, :] + 0.5 * _rms(o_ref[rows, :], gpost_ref[...])

        _for_row_chunks(o_ref.shape[0], chunk)


def _ffn(h, g_pre, wg, wu, wd, g_post, *, tm, tf):
    T, D = h.shape
    Fp = wg.shape[1]
    return pl.pallas_call(
        _ffn_kernel,
        grid=(T // tm, Fp // tf),
        in_specs=[
            pl.BlockSpec((tm, D), lambda i, j: (i, 0)),
            pl.BlockSpec((1, D), lambda i, j: (0, 0)),
            pl.BlockSpec((D, tf), lambda i, j: (0, j)),
            pl.BlockSpec((D, tf), lambda i, j: (0, j)),
            pl.BlockSpec((tf, D), lambda i, j: (j, 0)),
            pl.BlockSpec((1, D), lambda i, j: (0, 0)),
        ],
        out_specs=pl.BlockSpec((tm, D), lambda i, j: (i, 0)),
        out_shape=jax.ShapeDtypeStruct((T, D), F32),
        scratch_shapes=[pltpu.VMEM((tm, D), BF16)],
        compiler_params=_cparams(2),
        name="ffn",
    )(h, g_pre, wg, wu, wd, g_post)


def _proj_kernel(h_ref, g_ref, w_ref, wf_ref, o_ref, of_ref, xn_ref):
    @pl.when(pl.program_id(1) == 0)
    def _():
        _rms_to_bf16(h_ref, g_ref, xn_ref)
        of_ref[...] = jnp.dot(xn_ref[...], wf_ref[...], preferred_element_type=F32)

    o_ref[...] = jnp.dot(xn_ref[...], w_ref[...], preferred_element_type=F32).astype(o_ref.dtype)


def _proj(h, g, w, wf, *, tm, tn):
    T, D = h.shape
    N = w.shape[1]
    return pl.pallas_call(
        _proj_kernel,
        grid=(T // tm, N // tn),
        in_specs=[
            pl.BlockSpec((tm, D), lambda i, j: (i, 0)),
            pl.BlockSpec((1, D), lambda i, j: (0, 0)),
            pl.BlockSpec((D, tn), lambda i, j: (0, j)),
            pl.BlockSpec((D, LANES), lambda i, j: (0, 0)),
        ],
        out_specs=[
            pl.BlockSpec((tm, tn), lambda i, j: (i, j)),
            pl.BlockSpec((tm, LANES), lambda i, j: (i, 0)),
        ],
        out_shape=[jax.ShapeDtypeStruct((T, N), BF16), jax.ShapeDtypeStruct((T, LANES), F32)],
        scratch_shapes=[pltpu.VMEM((tm, D), BF16)],
        compiler_params=_cparams(2),
        name="mix_in_proj",
    )(h, g, w, wf)


def _rope_kernel(pos_ref, fr_ref, fm_ref, sr_ref, mc_ref, ms_ref, cr_ref, snr_ref, cm_ref, snm_ref):
    pos = pos_ref[...]
    ang_r = pos * fr_ref[...]
    cr_ref[...] = jnp.cos(ang_r)
    snr_ref[...] = jnp.sin(ang_r) * sr_ref[...]
    ang_m = pos * fm_ref[...]
    cm_ref[...] = jnp.cos(ang_m) * mc_ref[...]
    snm_ref[...] = jnp.sin(ang_m) * ms_ref[...]


def _rope_tables(pos_col, *, tm):
    T = pos_col.shape[0]
    lane = np.arange(LANES)
    half_r = HEAD_DIM // 2
    half_m = MLA_ROPE // 2
    freq_r = (ROPE_THETA ** (-(lane % half_r).astype(np.float32) / half_r)).astype(np.float32)
    sign_r = np.where(lane < half_r, -1.0, 1.0).astype(np.float32)
    active = (lane % 64) < half_m
    freq_m = (ROPE_THETA ** (-(lane % 64 % half_m).astype(np.float32) / half_m)).astype(np.float32)
    mask_c = active.astype(np.float32)
    mask_s = np.where(active, np.where(lane < 64, -1.0, 1.0), 0.0).astype(np.float32)
    consts = [jnp.asarray(c[None, :]) for c in (freq_r, freq_m, sign_r, mask_c, mask_s)]
    row = pl.BlockSpec((1, LANES), lambda i: (0, 0))
    tab = pl.BlockSpec((tm, LANES), lambda i: (i, 0))
    return pl.pallas_call(
        _rope_kernel,
        grid=(T // tm,),
        in_specs=[pl.BlockSpec((tm, 1), lambda i: (i, 0)), row, row, row, row, row],
        out_specs=[tab, tab, tab, tab],
        out_shape=[jax.ShapeDtypeStruct((T, LANES), F32)] * 4,
        compiler_params=_cparams(1),
        name="rope_tables",
    )(pos_col, *consts)


def _rot(x, cos, sin):
    return x * cos + pltpu.roll(x, LANES // 2, 1) * sin


def _ret_kernel(q_ref, k_ref, v_ref, g_ref, cos_ref, sin_ref, dm_ref, qd_ref, kd_ref, cd_ref,
                gain_ref, bias_ref, o_ref, st_ref, *, n_heads):
    @pl.when(pl.program_id(1) == 0)
    def _():
        st_ref[...] = jnp.zeros_like(st_ref)

    cos = cos_ref[...]
    sin = sin_ref[...]
    for h in range(n_heads):
        sl = slice(h * HEAD_DIM, (h + 1) * HEAD_DIM)
        qr = _rot(q_ref[:, sl].astype(F32), cos, sin)
        kr = _rot(k_ref[:, sl].astype(F32), cos, sin)
        vb = v_ref[:, sl]
        sc = lax.dot_general(qr.astype(BF16), kr.astype(BF16), (((1,), (1,)), ((), ())),
                             preferred_element_type=F32) * dm_ref[h]
        inner = jnp.dot(sc.astype(BF16), vb, preferred_element_type=F32)
        st = st_ref[h]
        cross = jnp.dot((qr * qd_ref[h]).astype(BF16), st.astype(BF16), preferred_element_type=F32)
        kdt = (kr * kd_ref[h]).T.astype(BF16)
        st_ref[h] = st * cd_ref[h] + jnp.dot(kdt, vb, preferred_element_type=F32)
        o = inner + cross
        mu = jnp.mean(o, axis=-1, keepdims=True)
        oc = o - mu
        var = jnp.mean(oc * oc, axis=-1, keepdims=True)
        y = (oc * lax.rsqrt(var + LN_EPS)) * gain_ref[:, sl] + bias_ref[:, sl]
        o_ref[:, sl] = (y * _silu(g_ref[:, sl].astype(F32))).astype(o_ref.dtype)


def _retention(proj, cos_r, sin_r, gain, bias, *, B, S, GW, C):
    T = B * S
    H = GW // HEAD_DIM
    nC = S // C
    idx = np.arange(C, dtype=np.float64)
    log_gamma = np.log(1.0 - 2.0 ** (-5.0 - np.arange(H, dtype=np.float64)))[:, None, None]
    diff = idx[:, None] - idx[None, :]
    dmat = np.where(diff >= 0, np.exp(log_gamma * np.maximum(diff, 0.0)), 0.0)
    qdec = np.broadcast_to(np.exp(log_gamma * (idx + 1.0)[None, :, None]), (H, C, LANES))
    kdec = np.broadcast_to(np.exp(log_gamma * (C - 1.0 - idx)[None, :, None]), (H, C, LANES))
    cdec = np.broadcast_to(np.exp(log_gamma * C), (H, 1, LANES))
    consts = [jnp.asarray(np.ascontiguousarray(c), F32) for c in (dmat, qdec, kdec, cdec)]

    def col(c):
        return pl.BlockSpec((C, GW), lambda b, i, c=c: (b * nC + i, c))

    def const(shape):
        return pl.BlockSpec(shape, lambda b, i: (0,) * len(shape))

    tab = pl.BlockSpec((C, LANES), lambda b, i: (b * nC + i, 0))
    return pl.pallas_call(
        functools.partial(_ret_kernel, n_heads=H),
        grid=(B, nC),
        in_specs=[col(0), col(1), col(2), col(3), tab, tab,
                  const((H, C, C)), const((H, C, LANES)), const((H, C, LANES)), const((H, 1, LANES)),
                  const((1, GW)), const((1, GW))],
        out_specs=pl.BlockSpec((C, GW), lambda b, i: (b * nC + i, 0)),
        out_shape=jax.ShapeDtypeStruct((T, GW), BF16),
        scratch_shapes=[pltpu.VMEM((H, HEAD_DIM, HEAD_DIM), F32)],
        compiler_params=_cparams(2),
        name="retention",
    )(proj, proj, proj, proj, cos_r, sin_r, *consts, gain, bias)


def _mla_prep_kernel(cq_ref, ckv_ref, kr_ref, gq_ref, gkv_ref, wq_ref, wkv_ref, cos_ref, sin_ref,
                     qn_ref, qr_ref, kn_ref, v_ref, kro_ref, *, gw):
    cos = cos_ref[...]
    sin = sin_ref[...]
    cqn = _rms(cq_ref[...].astype(F32), gq_ref[...]).astype(BF16)
    q = jnp.dot(cqn, wq_ref[...], preferred_element_type=F32)
    qn_ref[...] = q[:, :gw].astype(BF16)
    for h in range(gw // HEAD_DIM):
        sl = slice(h * HEAD_DIM, (h + 1) * HEAD_DIM)
        qr_ref[:, sl] = _rot(q[:, gw + h * HEAD_DIM:gw + (h + 1) * HEAD_DIM], cos, sin).astype(BF16)
    ckn = _rms(ckv_ref[...].astype(F32), gkv_ref[...]).astype(BF16)
    kv = jnp.dot(ckn, wkv_ref[...], preferred_element_type=F32)
    kn_ref[...] = kv[:, :gw].astype(BF16)
    v_ref[...] = kv[:, gw:].astype(BF16)
    kro_ref[...] = _rot(kr_ref[...].astype(F32), cos, sin).astype(BF16)


def _mla_prep(proj, gq, gkv, wq, wkv, cos_m, sin_m, *, GW, KVL, tm, off_cq, off_ckv, off_kr):
    T = proj.shape[0]
    row = lambda n: pl.BlockSpec((1, n), lambda i: (0, 0))
    blk = lambda n: pl.BlockSpec((tm, n), lambda i: (i, 0))
    return pl.pallas_call(
        functools.partial(_mla_prep_kernel, gw=GW),
        grid=(T // tm,),
        in_specs=[
            pl.BlockSpec((tm, GW), lambda i: (i, off_cq // GW)),
            pl.BlockSpec((tm, KVL), lambda i: (i, off_ckv // KVL)),
            pl.BlockSpec((tm, LANES), lambda i: (i, off_kr // LANES)),
            row(GW), row(KVL),
            pl.BlockSpec((GW, 2 * GW), lambda i: (0, 0)),
            pl.BlockSpec((KVL, 2 * GW), lambda i: (0, 0)),
            blk(LANES), blk(LANES),
        ],
        out_specs=[blk(GW), blk(GW), blk(GW), blk(GW), blk(LANES)],
        out_shape=[jax.ShapeDtypeStruct((T, GW), BF16)] * 4 + [jax.ShapeDtypeStruct((T, LANES), BF16)],
        compiler_params=_cparams(1),
        name="mla_prep",
    )(proj, proj, proj, gq, gkv, wq, wkv, cos_m, sin_m)


def _split3(x):
    hi = x.astype(BF16)
    r1 = x - hi.astype(F32)
    mid = r1.astype(BF16)
    lo = (r1 - mid.astype(F32)).astype(BF16)
    return hi, mid, lo


def _fox_prep_kernel(lg_ref, fb_ref, tri_ref, selq_ref, selk_ref, oneq_ref, onek_ref,
                     qe_ref, ke_ref, carry_ref, *, tiles_per_seq):
    @pl.when(pl.program_id(0) % tiles_per_seq == 0)
    def _():
        carry_ref[...] = jnp.zeros_like(carry_ref)

    x = lg_ref[...] + fb_ref[...]
    logf = jnp.minimum(x, 0.0) - jnp.log(1.0 + jnp.exp(-jnp.abs(x)))
    tri = tri_ref[...]
    cum = carry_ref[...]
    for part in _split3(logf):
        cum = cum + jnp.dot(tri, part, preferred_element_type=F32)
    carry_ref[...] = cum[cum.shape[0] - 1:, :]
    parts = jnp.concatenate(_split3(cum * LOG2E), axis=-1)
    qe_ref[...] = (jnp.dot(parts, selq_ref[...], preferred_element_type=F32) + oneq_ref[...]).astype(BF16)
    ke_ref[...] = (jnp.dot(parts, selk_ref[...], preferred_element_type=F32) + onek_ref[...]).astype(BF16)


def _fox_prep(logits, fbias, *, S, GW, tm):
    T = logits.shape[0]
    H = GW // HEAD_DIM
    tri = np.tril(np.ones((tm, tm), np.float32))
    selq = np.zeros((3 * LANES, GW), np.float32)
    selk = np.zeros((3 * LANES, GW), np.float32)
    oneq = np.zeros((1, GW), np.float32)
    onek = np.zeros((1, GW), np.float32)
    for h in range(H):
        for part in range(3):
            selq[part * LANES + h, h * HEAD_DIM + part] = 1.0
            selk[part * LANES + h, h * HEAD_DIM + 3 + part] = -1.0
            oneq[0, h * HEAD_DIM + 3 + part] = 1.0
            onek[0, h * HEAD_DIM + part] = 1.0
    const = lambda a: pl.BlockSpec(a.shape, lambda i: (0, 0))
    consts = [jnp.asarray(tri, BF16), jnp.asarray(selq, BF16), jnp.asarray(selk, BF16),
              jnp.asarray(oneq), jnp.asarray(onek)]
    return pl.pallas_call(
        functools.partial(_fox_prep_kernel, tiles_per_seq=S // tm),
        grid=(T // tm,),
        in_specs=[pl.BlockSpec((tm, LANES), lambda i: (i, 0)), pl.BlockSpec((1, LANES), lambda i: (0, 0))]
                 + [const(c) for c in consts],
        out_specs=[pl.BlockSpec((tm, GW), lambda i: (i, 0))] * 2,
        out_shape=[jax.ShapeDtypeStruct((T, GW), BF16)] * 2,
        scratch_shapes=[pltpu.VMEM((1, LANES), F32)],
        compiler_params=_cparams(1),
        name="fox_prep",
    )(logits, fbias, *consts)


ATTN_ROWS = 256
LOG2E = 1.4426950408889634


def _flash_kernel(qm_ref, qe_ref, km_ref, ke_ref, v_ref, o_ref, m_ref, acc_ref, *, tq, tk):
    i = pl.program_id(2)
    rg = min(ATTN_ROWS, tq)
    m_ref[...] = jnp.full_like(m_ref, NEG_BIG)
    acc_ref[...] = jnp.zeros_like(acc_ref)

    def chain(g, k_rows, n_k, mask_from):
        q_rows = slice(g * rg, (g + 1) * rg)
        q = jnp.concatenate([qm_ref[q_rows, :], qe_ref[q_rows, :]], axis=-1)
        k = jnp.concatenate([km_ref[k_rows, :], ke_ref[k_rows, :]], axis=-1)
        v = jnp.concatenate([v_ref[k_rows, :], jnp.ones((n_k, HEAD_DIM), BF16)], axis=-1)
        s = lax.dot_general(q, k, (((1,), (1,)), ((), ())), preferred_element_type=F32)
        if mask_from is not None:
            r = lax.broadcasted_iota(jnp.int32, (rg, n_k), 0)
            c = lax.broadcasted_iota(jnp.int32, (rg, n_k), 1)
            s = jnp.where(c <= r + mask_from, s, NEG_BIG)
        m_prev = m_ref[q_rows, :]
        m_new = jnp.maximum(m_prev, jnp.max(s, axis=-1, keepdims=True))
        alpha = jnp.exp2(m_prev - m_new)
        p = jnp.exp2(s - pltpu.repeat(m_new, n_k // LANES, axis=1))
        pv = jnp.dot(p.astype(BF16), v, preferred_element_type=F32)
        acc_ref[q_rows, :] = pltpu.repeat(alpha, 2, axis=1) * acc_ref[q_rows, :] + pv
        m_ref[q_rows, :] = m_new

    blocks_per_trip = tq // tk

    def body(j, carry):
        for u in range(blocks_per_trip):
            k_rows = pl.ds(pl.multiple_of((j * blocks_per_trip + u) * tk, tk), tk)
            for g in range(tq // rg):
                chain(g, k_rows, tk, None)
        return carry

    lax.fori_loop(0, i, body, 0)
    base = pl.multiple_of(i * tq, tq)
    for g in range(tq // rg):
        n_k = (g + 1) * rg
        chain(g, pl.ds(base, n_k), n_k, g * rg)
    acc = acc_ref[...]
    o_ref[...] = (acc[:, :HEAD_DIM] / acc[:, HEAD_DIM:]).astype(o_ref.dtype)


def _flash(qm, qe, km, ke, v, *, B, S, H, tq, tk, qm_off=0, km_off=0, v_off=0, ke_shared=False):
    T = B * S
    nq = S // tq
    qspec = lambda off: pl.BlockSpec((tq, HEAD_DIM), lambda b, h, i: (b * nq + i, off + h))
    kspec = lambda off: pl.BlockSpec((S, HEAD_DIM), lambda b, h, i: (b, off + h))
    ke_spec = pl.BlockSpec((S, HEAD_DIM), (lambda b, h, i: (b, 0)) if ke_shared else (lambda b, h, i: (b, h)))
    return pl.pallas_call(
        functools.partial(_flash_kernel, tq=tq, tk=tk),
        grid=(B, H, nq),
        in_specs=[qspec(qm_off), qspec(0), kspec(km_off), ke_spec, kspec(v_off)],
        out_specs=pl.BlockSpec((tq, HEAD_DIM), lambda b, h, i: (b * nq + i, h)),
        out_shape=jax.ShapeDtypeStruct((T, H * HEAD_DIM), BF16),
        scratch_shapes=[pltpu.VMEM((tq, LANES), F32), pltpu.VMEM((tq, 2 * HEAD_DIM), F32)],
        compiler_params=_cparams(3),
        name="causal_attention",
    )(qm, qe, km, ke, v)


def _dwconv_kernel(a_ref, b_ref, ba_ref, bb_ref, w_ref, wb_ref, o_ref, u_ref, halo_ref, *, tm, tiles_per_seq):
    c = pl.program_id(1)

    @pl.when(pl.program_id(0) % tiles_per_seq == 0)
    def _():
        halo_ref[c] = jnp.zeros((CONV_HALO, LANES), F32)

    u = (a_ref[...].astype(F32) + ba_ref[...]) * jax.nn.sigmoid(b_ref[...].astype(F32) + bb_ref[...])
    u_ref[:CONV_HALO, :] = halo_ref[c]
    u_ref[CONV_HALO:, :] = u
    halo_ref[c] = u[tm - CONV_HALO:, :]
    acc = jnp.zeros((tm, LANES), F32) + wb_ref[...]
    for j in range(CONV_K):
        start = CONV_HALO - (CONV_K - 1) + j
        acc = acc + u_ref[start:start + tm, :] * w_ref[j:j + 1, :]
    o_ref[...] = acc


def _dwconv(proj, glu_bias_a, glu_bias_b, w_dw, b_dw, *, S, GW, tm, off_a, off_b):
    T = proj.shape[0]
    nc = GW // LANES
    row = pl.BlockSpec((1, LANES), lambda i, c: (0, c))
    return pl.pallas_call(
        functools.partial(_dwconv_kernel, tm=tm, tiles_per_seq=S // tm),
        grid=(T // tm, nc),
        in_specs=[
            pl.BlockSpec((tm, LANES), lambda i, c: (i, off_a // LANES + c)),
            pl.BlockSpec((tm, LANES), lambda i, c: (i, off_b // LANES + c)),
            row, row,
            pl.BlockSpec((CONV_HALO, LANES), lambda i, c: (0, c)),
            row,
        ],
        out_specs=pl.BlockSpec((tm, LANES), lambda i, c: (i, c)),
        out_shape=jax.ShapeDtypeStruct((T, GW), F32),
        scratch_shapes=[pltpu.VMEM((tm + CONV_HALO, LANES), F32), pltpu.VMEM((nc, CONV_HALO, LANES), F32)],
        compiler_params=_cparams(2),
        name="glu_dwconv",
    )(proj, proj, glu_bias_a, glu_bias_b, w_dw, b_dw)


def _conv_pw_kernel(y_ref, g_ref, b_ref, w_ref, pb_ref, o_ref):
    y = y_ref[...]
    mu = jnp.mean(y, axis=-1, keepdims=True)
    yc = y - mu
    var = jnp.mean(yc * yc, axis=-1, keepdims=True)
    z = _silu(yc * lax.rsqrt(var + LN_EPS) * g_ref[...] + b_ref[...]).astype(BF16)
    o_ref[...] = (jnp.dot(z, w_ref[...], preferred_element_type=F32) + pb_ref[...]).astype(o_ref.dtype)


def _conv_pw(y, gain, bias, w_pw, pw_bias, *, tm):
    T, GW = y.shape
    row = pl.BlockSpec((1, GW), lambda i: (0, 0))
    blk = pl.BlockSpec((tm, GW), lambda i: (i, 0))
    return pl.pallas_call(
        _conv_pw_kernel,
        grid=(T // tm,),
        in_specs=[blk, row, row, pl.BlockSpec((GW, GW), lambda i: (0, 0)), row],
        out_specs=blk,
        out_shape=jax.ShapeDtypeStruct((T, GW), BF16),
        compiler_params=_cparams(1),
        name="conv_ln_pointwise",
    )(y, gain, bias, w_pw, pw_bias)


def _park_tile(o_ref, ss_ref, z, j, tn):
    o_ref[:, pl.ds(pl.multiple_of(j * tn, tn), tn)] = z
    ss_ref[...] += jnp.sum(z * z, axis=-1, keepdims=True)


def _finish_rows(h_ref, gain_ref, o_ref, ss_ref):
    d_model = o_ref.shape[1]

    def chunk(rows):
        r = lax.rsqrt(ss_ref[rows, :] * (1.0 / d_model) + NORM_EPS)
        o_ref[rows, :] = h_ref[rows, :] + o_ref[rows, :] * r * gain_ref[...]

    _for_row_chunks(o_ref.shape[0], chunk)


def _out_proj_kernel(y0_ref, y1_ref, y2_ref, y3_ref, w_ref, h_ref, g_ref, o_ref, ss_ref,
                     *, gw, tn, n_tiles):
    j = pl.program_id(1)

    @pl.when(j == 0)
    def _():
        ss_ref[...] = jnp.zeros_like(ss_ref)

    z = jnp.dot(y0_ref[...], w_ref[0:gw, :], preferred_element_type=F32)
    z += jnp.dot(y1_ref[...], w_ref[gw:2 * gw, :], preferred_element_type=F32)
    z += jnp.dot(y2_ref[...], w_ref[2 * gw:3 * gw, :], preferred_element_type=F32)
    z += jnp.dot(y3_ref[...], w_ref[3 * gw:4 * gw, :], preferred_element_type=F32)
    _park_tile(o_ref, ss_ref, z, j, tn)

    @pl.when(j == n_tiles - 1)
    def _():
        _finish_rows(h_ref, g_ref, o_ref, ss_ref)


def _out_proj(ys, w, h, g, *, tm, tn):
    T, D = h.shape
    GW = ys[0].shape[1]
    n_tiles = D // tn
    yspec = pl.BlockSpec((tm, GW), lambda i, j: (i, 0))
    return pl.pallas_call(
        functools.partial(_out_proj_kernel, gw=GW, tn=tn, n_tiles=n_tiles),
        grid=(T // tm, n_tiles),
        in_specs=[yspec, yspec, yspec, yspec,
                  pl.BlockSpec((N_GROUPS * GW, tn), lambda i, j: (0, j)),
                  pl.BlockSpec((tm, D), lambda i, j: (i, 0)),
                  pl.BlockSpec((1, D), lambda i, j: (0, 0))],
        out_specs=pl.BlockSpec((tm, D), lambda i, j: (i, 0)),
        out_shape=jax.ShapeDtypeStruct((T, D), F32),
        scratch_shapes=[pltpu.VMEM((tm, 1), F32)],
        compiler_params=_cparams(2),
        name="mix_out_proj",
    )(*ys, w, h, g)


def _ple_kernel(h_ref, gin_ref, wg_ref, p_ref, wp_ref, gpost_ref, o_ref, xn_ref, ss_ref, *, tn, n_tiles):
    j = pl.program_id(1)

    @pl.when(j == 0)
    def _():
        _rms_to_bf16(h_ref, gin_ref, xn_ref)
        ss_ref[...] = jnp.zeros_like(ss_ref)

    gate = jax.nn.sigmoid(jnp.dot(xn_ref[...], wg_ref[...], preferred_element_type=F32))
    z = gate * jnp.dot(p_ref[...], wp_ref[...], preferred_element_type=F32)
    _park_tile(o_ref, ss_ref, z, j, tn)

    @pl.when(j == n_tiles - 1)
    def _():
        _finish_rows(h_ref, gpost_ref, o_ref, ss_ref)


def _ple(h, g_in, wg, p, wp, g_post, *, tm, tn):
    T, D = h.shape
    P = p.shape[1]
    n_tiles = D // tn
    return pl.pallas_call(
        functools.partial(_ple_kernel, tn=tn, n_tiles=n_tiles),
        grid=(T // tm, n_tiles),
        in_specs=[pl.BlockSpec((tm, D), lambda i, j: (i, 0)),
                  pl.BlockSpec((1, D), lambda i, j: (0, 0)),
                  pl.BlockSpec((D, tn), lambda i, j: (0, j)),
                  pl.BlockSpec((tm, P), lambda i, j: (i, 0)),
                  pl.BlockSpec((P, tn), lambda i, j: (0, j)),
                  pl.BlockSpec((1, D), lambda i, j: (0, 0))],
        out_specs=pl.BlockSpec((tm, D), lambda i, j: (i, 0)),
        out_shape=jax.ShapeDtypeStruct((T, D), F32),
        scratch_shapes=[pltpu.VMEM((tm, D), BF16), pltpu.VMEM((tm, 1), F32)],
        compiler_params=_cparams(2),
        name="ple",
    )(h, g_in, wg, p, wp, g_post)


def _rope_lane_layout(w, half):
    z = jnp.zeros((w.shape[0], LANES // 2 - half), w.dtype)
    return jnp.concatenate([w[:, :half], z, w[:, half:], z], axis=1)


def _pad_cols(w, n):
    return jnp.pad(w, ((0, 0), (0, n - w.shape[1])))


def _plan(D, F, S, T):
    return dict(
        tm=_tile(T if T < S else S, 512),
        tf=_tile(F, 256),
        tn=_tile(D, 512),
        tn_proj=1024,
        tq=_tile(S, 1024),
        tk=_tile(S, 512),
        c_ret=_tile(S, 256),
    )


def kernel(x, p, positions, ffn1_norm_pre, ffn1_w_gate, ffn1_w_up, ffn1_w_down, ffn1_norm_post, mix_norm_pre, w_in, mla_q_norm, mla_w_uq, mla_kv_norm, mla_w_ukv, ret_gn_gain, ret_gn_bias, fox_forget_bias, conv_glu_bias, conv_dw, conv_dw_bias, conv_ln_gain, conv_ln_bias, conv_w_pw, conv_pw_bias, w_out, mix_norm_post, ffn2_norm_pre, ffn2_w_gate, ffn2_w_up, ffn2_w_down, ffn2_norm_post, ple_norm_in, ple_w_gate, ple_w_proj, ple_norm_post):
    B, S, D = x.shape
    depth = p.shape[0]
    T = B * S
    F = ffn1_w_gate.shape[2]
    GW = D // N_GROUPS
    H = GW // HEAD_DIM
    KVL = mla_kv_norm.shape[1]
    NOPE = HEAD_DIM
    plan = _plan(D, F, S, T)
    tm, tf, tn, tq, tk, c_ret = (plan[n] for n in ("tm", "tf", "tn", "tq", "tk", "c_ret"))
    row = lambda v: v.reshape(1, -1).astype(F32)

    h = x.reshape(T, D)
    pos_col = positions.reshape(T, 1).astype(F32)
    cos_r, sin_r, cos_m, sin_m = _rope_tables(pos_col, tm=tm)

    sp = np.cumsum([GW] * 4 + [GW, KVL, MLA_ROPE] + [GW] * 3 + [H, 2 * GW])
    off_cq, off_ckv, off_kr = 4 * GW, 5 * GW, 5 * GW + KVL
    off_fox = off_kr + LANES
    off_conv = off_fox + 3 * GW
    n_main = off_conv + 2 * GW
    tn_proj = plan["tn_proj"]
    n_main_pad = -(-n_main // tn_proj) * tn_proj
    assert mla_q_norm.shape[1] == GW and off_ckv % KVL == 0 and KVL % LANES == 0
    mla_scale = (NOPE + MLA_ROPE) ** -0.5 * LOG2E
    col_scale = np.ones((int(sp[-1]),), np.float32)
    col_scale[sp[0]:sp[1]] = HEAD_DIM ** -0.5
    col_scale[sp[6]:sp[7]] = HEAD_DIM ** -0.5 * LOG2E

    for i in range(depth):
        wb = (w_in[i] * col_scale).astype(BF16)
        w_main = jnp.concatenate(
            [wb[:, :sp[5]], _rope_lane_layout(wb[:, sp[5]:sp[6]], MLA_ROPE // 2), wb[:, sp[6]:sp[9]],
             wb[:, sp[10]:sp[11]], jnp.zeros((D, n_main_pad - n_main), BF16)], axis=1)
        w_f = _pad_cols(wb[:, sp[9]:sp[10]], LANES)
        uq = mla_w_uq[i].reshape(GW, H, NOPE + MLA_ROPE) * mla_scale
        uq_nope = uq[:, :, :NOPE].reshape(GW, H * NOPE)
        uq_rope = jnp.concatenate(
            [_rope_lane_layout(uq[:, hh, NOPE:], MLA_ROPE // 2) for hh in range(H)], axis=1)
        w_uq = jnp.concatenate([uq_nope, uq_rope], axis=1).astype(BF16)
        ukv = mla_w_ukv[i].reshape(KVL, H, 2 * HEAD_DIM)
        w_ukv = jnp.concatenate([ukv[:, :, :NOPE].reshape(KVL, GW), ukv[:, :, NOPE:].reshape(KVL, GW)],
                                axis=1).astype(BF16)
        ffn_w = [tuple(_to_bf16(w_, i) for w_ in ws)
                 for ws in ((ffn1_w_gate, ffn1_w_up, ffn1_w_down), (ffn2_w_gate, ffn2_w_up, ffn2_w_down))]
        dw = jnp.pad(conv_dw[i], ((0, CONV_HALO - CONV_K), (0, 0)))

        h = _ffn(h, row(ffn1_norm_pre[i]), *ffn_w[0], row(ffn1_norm_post[i]), tm=tm, tf=tf)

        proj, logits = _proj(h, row(mix_norm_pre[i]), w_main, w_f, tm=tm, tn=tn_proj)
        y_ret = _retention(proj, cos_r, sin_r, row(ret_gn_gain[i]), row(ret_gn_bias[i]), B=B, S=S, GW=GW, C=c_ret)
        qn, qr, kn, mv, kro = _mla_prep(proj, row(mla_q_norm[i]), row(mla_kv_norm[i]), w_uq, w_ukv, cos_m, sin_m,
                                        GW=GW, KVL=KVL, tm=tm, off_cq=off_cq, off_ckv=off_ckv, off_kr=off_kr)
        y_mla = _flash(qn, qr, kn, kro, mv, B=B, S=S, H=H, tq=tq, tk=tk, ke_shared=True)
        fbias = jnp.pad(fox_forget_bias[i], (0, LANES - H)).reshape(1, LANES)
        qe, ke = _fox_prep(logits, fbias, S=S, GW=GW, tm=tm)
        hb = off_fox // HEAD_DIM
        y_fox = _flash(proj, qe, proj, ke, proj, B=B, S=S, H=H, tq=tq, tk=tk,
                       qm_off=hb, km_off=hb + H, v_off=hb + 2 * H)
        gb = conv_glu_bias[i]
        y_dw = _dwconv(proj, row(gb[:GW]), row(gb[GW:]), dw, row(conv_dw_bias[i]),
                       S=S, GW=GW, tm=tm, off_a=off_conv, off_b=off_conv + GW)
        y_conv = _conv_pw(y_dw, row(conv_ln_gain[i]), row(conv_ln_bias[i]), conv_w_pw[i].astype(BF16),
                          row(conv_pw_bias[i]), tm=tm)
        h = _out_proj([y_ret, y_mla, y_fox, y_conv], _to_bf16(w_out, i), h, row(mix_norm_post[i]), tm=tm, tn=tn)

        h = _ffn(h, row(ffn2_norm_pre[i]), *ffn_w[1], row(ffn2_norm_post[i]), tm=tm, tf=tf)

        h = _ple(h, row(ple_norm_in[i]), _to_bf16(ple_w_gate, i), p[i].reshape(T, -1).astype(BF16),
                 ple_w_proj[i].astype(BF16), row(ple_norm_post[i]), tm=tm, tn=tn)

    return h.reshape(B, S, D)
```

```python
import functools

import numpy as np
import jax
import jax.numpy as jnp
from jax import lax
from jax.experimental import pallas as pl
from jax.experimental.pallas import tpu as pltpu

F32 = jnp.float32
BF16 = jnp.bfloat16

HEAD_DIM = 128
LANES = 128
N_GROUPS = 4
MLA_ROPE = 64
CONV_K = 31
CONV_HALO = 32
ROPE_THETA = 10000.0
NORM_EPS = 1e-6
LN_EPS = 1e-5
NEG_BIG = -1e30
VMEM_LIMIT = 60 * 1024 * 1024


def _cparams(n_axes):
    return pltpu.CompilerParams(dimension_semantics=("arbitrary",) * n_axes,
                                vmem_limit_bytes=VMEM_LIMIT)


def _tile(n, pref):
    if n <= pref:
        return n
    t = pref - pref % LANES
    while t >= LANES:
        if n % t == 0:
            return t
        t -= LANES
    raise ValueError(f"no lane-aligned tile for {n}")


def _rms(x, gain, eps=NORM_EPS):
    return x * lax.rsqrt(jnp.mean(x * x, axis=-1, keepdims=True) + eps) * gain


def _silu(x):
    return x * jax.nn.sigmoid(x)


ROW_CHUNK = 64
COL_CHUNK = 512


def _for_row_chunks(n_rows, fn):
    chunk = min(ROW_CHUNK, n_rows)

    def body(r, carry):
        fn(pl.ds(pl.multiple_of(r * chunk, chunk), chunk))
        return carry

    lax.fori_loop(0, n_rows // chunk, body, 0)


def _col_slices(n_cols):
    cc = min(COL_CHUNK, n_cols)
    return [slice(c * cc, (c + 1) * cc) for c in range(n_cols // cc)]


def _row_inv_rms(src_ref, rows):
    n_cols = src_ref.shape[1]
    ss = None
    for sl in _col_slices(n_cols):
        x = src_ref[rows, sl]
        part = jnp.sum(x * x, axis=-1, keepdims=True)
        ss = part if ss is None else ss + part
    return lax.rsqrt(ss * (1.0 / n_cols) + NORM_EPS)


def _rms_to_bf16(src_ref, gain_ref, dst_ref):
    def chunk(rows):
        r = _row_inv_rms(src_ref, rows)
        for sl in _col_slices(src_ref.shape[1]):
            dst_ref[rows, sl] = (src_ref[rows, sl] * r * gain_ref[:, sl]).astype(BF16)

    _for_row_chunks(src_ref.shape[0], chunk)


def _add_scaled_rows(h_ref, gain_ref, o_ref, rows, r):
    for sl in _col_slices(o_ref.shape[1]):
        o_ref[rows, sl] = h_ref[rows, sl] + o_ref[rows, sl] * r * gain_ref[:, sl]


CAST_BLOCK_BYTES = 8 * 1024 * 1024
BF16_SUBLANES = 16


def _cast_kernel(x_ref, o_ref):
    o_ref[...] = x_ref[...].astype(BF16)


def _to_bf16(w_stack, layer):
    _, R, C = w_stack.shape
    rows = R
    for cand in range(BF16_SUBLANES, R + 1, BF16_SUBLANES):
        if R % cand == 0 and cand * C * 4 <= CAST_BLOCK_BYTES:
            rows = cand
    if rows == R and R * C * 4 > CAST_BLOCK_BYTES:
        return w_stack[layer].astype(BF16)
    return pl.pallas_call(
        _cast_kernel,
        grid=(R // rows,),
        in_specs=[pl.BlockSpec((None, rows, C), lambda r: (layer, r, 0))],
        out_specs=pl.BlockSpec((rows, C), lambda r: (r, 0)),
        out_shape=jax.ShapeDtypeStruct((R, C), BF16),
        compiler_params=_cparams(1),
        name="cast_bf16",
    )(w_stack)


W_IN_ROWS = 128


def _mix_in_weights_kernel(x_ref, s_ref, o_ref, f_ref, *, sp, off_kr, n_main):
    rows = x_ref.shape[0]
    half = MLA_ROPE // 2

    def piece(a, b):
        return (x_ref[:, a:b] * s_ref[:, a:b]).astype(BF16)

    o_ref[:, :sp[5]] = piece(0, sp[5])
    kr = piece(sp[5], sp[6])
    gap = jnp.zeros((rows, LANES // 2 - half), BF16)
    o_ref[:, off_kr:off_kr + LANES] = jnp.concatenate([kr[:, :half], gap, kr[:, half:], gap], axis=1)
    off_fox = off_kr + LANES
    off_conv = off_fox + sp[9] - sp[6]
    o_ref[:, off_fox:off_conv] = piece(sp[6], sp[9])
    o_ref[:, off_conv:n_main] = piece(sp[10], sp[11])
    if o_ref.shape[1] > n_main:
        o_ref[:, n_main:] = jnp.zeros((rows, o_ref.shape[1] - n_main), BF16)
    n_f = sp[10] - sp[9]
    f_ref[...] = jnp.concatenate([piece(sp[9], sp[10]), jnp.zeros((rows, LANES - n_f), BF16)], axis=1)


def _mix_in_weights(w_in, col_scale, layer, *, sp, off_kr, n_main, n_pad):
    _, D, n_in = w_in.shape
    rows = min(W_IN_ROWS, D)
    return pl.pallas_call(
        functools.partial(_mix_in_weights_kernel, sp=sp, off_kr=off_kr, n_main=n_main),
        grid=(D // rows,),
        in_specs=[pl.BlockSpec((None, rows, n_in), lambda r: (layer, r, 0)),
                  pl.BlockSpec((1, n_in), lambda r: (0, 0))],
        out_specs=[pl.BlockSpec((rows, n_pad), lambda r: (r, 0)), pl.BlockSpec((rows, LANES), lambda r: (r, 0))],
        out_shape=[jax.ShapeDtypeStruct((D, n_pad), BF16), jax.ShapeDtypeStruct((D, LANES), BF16)],
        compiler_params=_cparams(1),
        name="mix_in_weights",
    )(w_in, col_scale)


def _ffn_kernel(h_ref, gpre_ref, wg_ref, wu_ref, wd_ref, gpost_ref, o_ref, xn_ref):
    j = pl.program_id(1)

    @pl.when(j == 0)
    def _():
        _rms_to_bf16(h_ref, gpre_ref, xn_ref)
        o_ref[...] = jnp.zeros_like(o_ref)

    xn = xn_ref[...]
    g = jnp.dot(xn, wg_ref[...], preferred_element_type=F32)
    u = jnp.dot(xn, wu_ref[...], preferred_element_type=F32)
    a = (_silu(g) * u).astype(BF16)
    d_model = o_ref.shape[1]
    cc = min(COL_CHUNK, d_model)
    for n in range(d_model // cc):
        sl = slice(n * cc, (n + 1) * cc)
        o_ref[:, sl] += jnp.dot(a, wd_ref[:, sl], preferred_element_type=F32)

    @pl.when(j == pl.num_programs(1) - 1)
    def _():
        def chunk(rows):
            _add_scaled_rows(h_ref, gpost_ref, o_ref, rows, 0.5 * _row_inv_rms(o_ref, rows))

        _for_row_chunks(o_ref.shape[0], chunk)


def _ffn(h, g_pre, wg, wu, wd, g_post, *, tm, tf):
    T, D = h.shape
    Fp = wg.shape[1]
    return pl.pallas_call(
        _ffn_kernel,
        grid=(T // tm, Fp // tf),
        in_specs=[
            pl.BlockSpec((tm, D), lambda i, j: (i, 0)),
            pl.BlockSpec((1, D), lambda i, j: (0, 0)),
            pl.BlockSpec((D, tf), lambda i, j: (0, j)),
            pl.BlockSpec((D, tf), lambda i, j: (0, j)),
            pl.BlockSpec((tf, D), lambda i, j: (j, 0)),
            pl.BlockSpec((1, D), lambda i, j: (0, 0)),
        ],
        out_specs=pl.BlockSpec((tm, D), lambda i, j: (i, 0)),
        out_shape=jax.ShapeDtypeStruct((T, D), F32),
        scratch_shapes=[pltpu.VMEM((tm, D), BF16)],
        compiler_params=_cparams(2),
        name="ffn",
    )(h, g_pre, wg, wu, wd, g_post)


def _proj_kernel(h_ref, g_ref, w_ref, wf_ref, o_ref, of_ref, xn_ref):
    @pl.when(pl.program_id(1) == 0)
    def _():
        _rms_to_bf16(h_ref, g_ref, xn_ref)
        of_ref[...] = jnp.dot(xn_ref[...], wf_ref[...], preferred_element_type=F32)

    o_ref[...] = jnp.dot(xn_ref[...], w_ref[...], preferred_element_type=F32).astype(o_ref.dtype)


def _proj(h, g, w, wf, *, tm, tn):
    T, D = h.shape
    N = w.shape[1]
    return pl.pallas_call(
        _proj_kernel,
        grid=(T // tm, N // tn),
        in_specs=[
            pl.BlockSpec((tm, D), lambda i, j: (i, 0)),
            pl.BlockSpec((1, D), lambda i, j: (0, 0)),
            pl.BlockSpec((D, tn), lambda i, j: (0, j)),
            pl.BlockSpec((D, LANES), lambda i, j: (0, 0)),
        ],
        out_specs=[
            pl.BlockSpec((tm, tn), lambda i, j: (i, j)),
            pl.BlockSpec((tm, LANES), lambda i, j: (i, 0)),
        ],
        out_shape=[jax.ShapeDtypeStruct((T, N), BF16), jax.ShapeDtypeStruct((T, LANES), F32)],
        scratch_shapes=[pltpu.VMEM((tm, D), BF16)],
        compiler_params=_cparams(2),
        name="mix_in_proj",
    )(h, g, w, wf)


def _rope_kernel(pos_ref, fr_ref, fm_ref, sr_ref, mc_ref, ms_ref, cr_ref, snr_ref, cm_ref, snm_ref):
    pos = pos_ref[...]
    ang_r = pos * fr_ref[...]
    cr_ref[...] = jnp.cos(ang_r)
    snr_ref[...] = jnp.sin(ang_r) * sr_ref[...]
    ang_m = pos * fm_ref[...]
    cm_ref[...] = jnp.cos(ang_m) * mc_ref[...]
    snm_ref[...] = jnp.sin(ang_m) * ms_ref[...]


def _rope_tables(pos_col, *, tm):
    T = pos_col.shape[0]
    lane = np.arange(LANES)
    half_r = HEAD_DIM // 2
    half_m = MLA_ROPE // 2
    freq_r = (ROPE_THETA ** (-(lane % half_r).astype(np.float32) / half_r)).astype(np.float32)
    sign_r = np.where(lane < half_r, -1.0, 1.0).astype(np.float32)
    active = (lane % 64) < half_m
    freq_m = (ROPE_THETA ** (-(lane % 64 % half_m).astype(np.float32) / half_m)).astype(np.float32)
    mask_c = active.astype(np.float32)
    mask_s = np.where(active, np.where(lane < 64, -1.0, 1.0), 0.0).astype(np.float32)
    consts = [jnp.asarray(c[None, :]) for c in (freq_r, freq_m, sign_r, mask_c, mask_s)]
    row = pl.BlockSpec((1, LANES), lambda i: (0, 0))
    tab = pl.BlockSpec((tm, LANES), lambda i: (i, 0))
    return pl.pallas_call(
        _rope_kernel,
        grid=(T // tm,),
        in_specs=[pl.BlockSpec((tm, 1), lambda i: (i, 0)), row, row, row, row, row],
        out_specs=[tab, tab, tab, tab],
        out_shape=[jax.ShapeDtypeStruct((T, LANES), F32)] * 4,
        compiler_params=_cparams(1),
        name="rope_tables",
    )(pos_col, *consts)


def _rot(x, cos, sin):
    return x * cos + pltpu.roll(x, LANES // 2, 1) * sin


def _ret_kernel(q_ref, k_ref, v_ref, g_ref, cos_ref, sin_ref, dm_ref, qd_ref, kd_ref, cd_ref,
                gain_ref, bias_ref, o_ref, st_ref, *, n_heads):
    @pl.when(pl.program_id(1) == 0)
    def _():
        st_ref[...] = jnp.zeros_like(st_ref)

    cos = cos_ref[...]
    sin = sin_ref[...]
    for h in range(n_heads):
        sl = slice(h * HEAD_DIM, (h + 1) * HEAD_DIM)
        qr = _rot(q_ref[:, sl].astype(F32), cos, sin)
        kr = _rot(k_ref[:, sl].astype(F32), cos, sin)
        vb = v_ref[:, sl]
        sc = lax.dot_general(qr.astype(BF16), kr.astype(BF16), (((1,), (1,)), ((), ())),
                             preferred_element_type=F32) * dm_ref[h]
        inner = jnp.dot(sc.astype(BF16), vb, preferred_element_type=F32)
        st = st_ref[h]
        cross = jnp.dot((qr * qd_ref[h]).astype(BF16), st.astype(BF16), preferred_element_type=F32)
        kdt = (kr * kd_ref[h]).T.astype(BF16)
        st_ref[h] = st * cd_ref[h] + jnp.dot(kdt, vb, preferred_element_type=F32)
        o = inner + cross
        mu = jnp.mean(o, axis=-1, keepdims=True)
        oc = o - mu
        var = jnp.mean(oc * oc, axis=-1, keepdims=True)
        y = (oc * lax.rsqrt(var + LN_EPS)) * gain_ref[:, sl] + bias_ref[:, sl]
        o_ref[:, sl] = (y * _silu(g_ref[:, sl].astype(F32))).astype(o_ref.dtype)


def _retention(proj, cos_r, sin_r, gain, bias, *, B, S, GW, C):
    T = B * S
    H = GW // HEAD_DIM
    nC = S // C
    idx = np.arange(C, dtype=np.float64)
    log_gamma = np.log(1.0 - 2.0 ** (-5.0 - np.arange(H, dtype=np.float64)))[:, None, None]
    diff = idx[:, None] - idx[None, :]
    dmat = np.where(diff >= 0, np.exp(log_gamma * np.maximum(diff, 0.0)), 0.0)
    qdec = np.broadcast_to(np.exp(log_gamma * (idx + 1.0)[None, :, None]), (H, C, LANES))
    kdec = np.broadcast_to(np.exp(log_gamma * (C - 1.0 - idx)[None, :, None]), (H, C, LANES))
    cdec = np.broadcast_to(np.exp(log_gamma * C), (H, 1, LANES))
    consts = [jnp.asarray(np.ascontiguousarray(c), F32) for c in (dmat, qdec, kdec, cdec)]

    def col(c):
        return pl.BlockSpec((C, GW), lambda b, i, c=c: (b * nC + i, c))

    def const(shape):
        return pl.BlockSpec(shape, lambda b, i: (0,) * len(shape))

    tab = pl.BlockSpec((C, LANES), lambda b, i: (b * nC + i, 0))
    return pl.pallas_call(
        functools.partial(_ret_kernel, n_heads=H),
        grid=(B, nC),
        in_specs=[col(0), col(1), col(2), col(3), tab, tab,
                  const((H, C, C)), const((H, C, LANES)), const((H, C, LANES)), const((H, 1, LANES)),
                  const((1, GW)), const((1, GW))],
        out_specs=pl.BlockSpec((C, GW), lambda b, i: (b * nC + i, 0)),
        out_shape=jax.ShapeDtypeStruct((T, GW), BF16),
        scratch_shapes=[pltpu.VMEM((H, HEAD_DIM, HEAD_DIM), F32)],
        compiler_params=_cparams(2),
        name="retention",
    )(proj, proj, proj, proj, cos_r, sin_r, *consts, gain, bias)


def _mla_prep_kernel(cq_ref, ckv_ref, kr_ref, gq_ref, gkv_ref, wq_ref, wkv_ref, cos_ref, sin_ref,
                     qn_ref, qr_ref, kn_ref, v_ref, kro_ref, *, gw):
    cos = cos_ref[...]
    sin = sin_ref[...]
    cqn = _rms(cq_ref[...].astype(F32), gq_ref[...]).astype(BF16)
    q = jnp.dot(cqn, wq_ref[...], preferred_element_type=F32)
    qn_ref[...] = q[:, :gw].astype(BF16)
    for h in range(gw // HEAD_DIM):
        sl = slice(h * HEAD_DIM, (h + 1) * HEAD_DIM)
        qr_ref[:, sl] = _rot(q[:, gw + h * HEAD_DIM:gw + (h + 1) * HEAD_DIM], cos, sin).astype(BF16)
    ckn = _rms(ckv_ref[...].astype(F32), gkv_ref[...]).astype(BF16)
    kv = jnp.dot(ckn, wkv_ref[...], preferred_element_type=F32)
    kn_ref[...] = kv[:, :gw].astype(BF16)
    v_ref[...] = kv[:, gw:].astype(BF16)
    kro_ref[...] = _rot(kr_ref[...].astype(F32), cos, sin).astype(BF16)


def _mla_prep(proj, gq, gkv, wq, wkv, cos_m, sin_m, *, GW, KVL, tm, off_cq, off_ckv, off_kr):
    T = proj.shape[0]
    row = lambda n: pl.BlockSpec((1, n), lambda i: (0, 0))
    blk = lambda n: pl.BlockSpec((tm, n), lambda i: (i, 0))
    return pl.pallas_call(
        functools.partial(_mla_prep_kernel, gw=GW),
        grid=(T // tm,),
        in_specs=[
            pl.BlockSpec((tm, GW), lambda i: (i, off_cq // GW)),
            pl.BlockSpec((tm, KVL), lambda i: (i, off_ckv // KVL)),
            pl.BlockSpec((tm, LANES), lambda i: (i, off_kr // LANES)),
            row(GW), row(KVL),
            pl.BlockSpec((GW, 2 * GW), lambda i: (0, 0)),
            pl.BlockSpec((KVL, 2 * GW), lambda i: (0, 0)),
            blk(LANES), blk(LANES),
        ],
        out_specs=[blk(GW), blk(GW), blk(GW), blk(GW), blk(LANES)],
        out_shape=[jax.ShapeDtypeStruct((T, GW), BF16)] * 4 + [jax.ShapeDtypeStruct((T, LANES), BF16)],
        compiler_params=_cparams(1),
        name="mla_prep",
    )(proj, proj, proj, gq, gkv, wq, wkv, cos_m, sin_m)


def _split3(x):
    hi = x.astype(BF16)
    r1 = x - hi.astype(F32)
    mid = r1.astype(BF16)
    lo = (r1 - mid.astype(F32)).astype(BF16)
    return hi, mid, lo


def _fox_prep_kernel(lg_ref, fb_ref, tri_ref, selq_ref, selk_ref, oneq_ref, onek_ref,
                     qe_ref, ke_ref, carry_ref, *, tiles_per_seq):
    @pl.when(pl.program_id(0) % tiles_per_seq == 0)
    def _():
        carry_ref[...] = jnp.zeros_like(carry_ref)

    x = lg_ref[...] + fb_ref[...]
    logf = jnp.minimum(x, 0.0) - jnp.log(1.0 + jnp.exp(-jnp.abs(x)))
    tri = tri_ref[...]
    cum = carry_ref[...]
    for part in _split3(logf):
        cum = cum + jnp.dot(tri, part, preferred_element_type=F32)
    carry_ref[...] = cum[cum.shape[0] - 1:, :]
    parts = jnp.concatenate(_split3(cum * LOG2E), axis=-1)
    qe_ref[...] = (jnp.dot(parts, selq_ref[...], preferred_element_type=F32) + oneq_ref[...]).astype(BF16)
    ke_ref[...] = (jnp.dot(parts, selk_ref[...], preferred_element_type=F32) + onek_ref[...]).astype(BF16)


def _fox_prep(logits, fbias, *, S, GW, tm):
    T = logits.shape[0]
    H = GW // HEAD_DIM
    tri = np.tril(np.ones((tm, tm), np.float32))
    selq = np.zeros((3 * LANES, GW), np.float32)
    selk = np.zeros((3 * LANES, GW), np.float32)
    oneq = np.zeros((1, GW), np.float32)
    onek = np.zeros((1, GW), np.float32)
    for h in range(H):
        for part in range(3):
            selq[part * LANES + h, h * HEAD_DIM + part] = 1.0
            selk[part * LANES + h, h * HEAD_DIM + 3 + part] = -1.0
            oneq[0, h * HEAD_DIM + 3 + part] = 1.0
            onek[0, h * HEAD_DIM + part] = 1.0
    const = lambda a: pl.BlockSpec(a.shape, lambda i: (0, 0))
    consts = [jnp.asarray(tri, BF16), jnp.asarray(selq, BF16), jnp.asarray(selk, BF16),
              jnp.asarray(oneq), jnp.asarray(onek)]
    return pl.pallas_call(
        functools.partial(_fox_prep_kernel, tiles_per_seq=S // tm),
        grid=(T // tm,),
        in_specs=[pl.BlockSpec((tm, LANES), lambda i: (i, 0)), pl.BlockSpec((1, LANES), lambda i: (0, 0))]
                 + [const(c) for c in consts],
        out_specs=[pl.BlockSpec((tm, GW), lambda i: (i, 0))] * 2,
        out_shape=[jax.ShapeDtypeStruct((T, GW), BF16)] * 2,
        scratch_shapes=[pltpu.VMEM((1, LANES), F32)],
        compiler_params=_cparams(1),
        name="fox_prep",
    )(logits, fbias, *consts)


ATTN_ROWS = 256
LOG2E = 1.4426950408889634


def _lane_tile(x, n):
    return x if n == 1 else jnp.concatenate([x] * n, axis=1)


def _flash_kernel(qm_ref, qe_ref, km_ref, ke_ref, v_ref, o_ref, m_ref, acc_ref, *, tq, tk):
    i = pl.program_id(2)
    rg = min(ATTN_ROWS, tq)
    m_ref[...] = jnp.full_like(m_ref, NEG_BIG)
    acc_ref[...] = jnp.zeros_like(acc_ref)

    def chain(g, k_rows, n_k, mask_from):
        q_rows = slice(g * rg, (g + 1) * rg)
        q = jnp.concatenate([qm_ref[q_rows, :], qe_ref[q_rows, :]], axis=-1)
        k = jnp.concatenate([km_ref[k_rows, :], ke_ref[k_rows, :]], axis=-1)
        v = jnp.concatenate([v_ref[k_rows, :], jnp.ones((n_k, HEAD_DIM), BF16)], axis=-1)
        s = lax.dot_general(q, k, (((1,), (1,)), ((), ())), preferred_element_type=F32)
        if mask_from is not None:
            r = lax.broadcasted_iota(jnp.int32, (rg, n_k), 0)
            c = lax.broadcasted_iota(jnp.int32, (rg, n_k), 1)
            s = jnp.where(c <= r + mask_from, s, NEG_BIG)
        m_prev = m_ref[q_rows, :]
        m_new = jnp.maximum(m_prev, jnp.max(s, axis=-1, keepdims=True))
        alpha = jnp.exp2(m_prev - m_new)
        p = jnp.exp2(s - _lane_tile(m_new, n_k // LANES))
        pv = jnp.dot(p.astype(BF16), v, preferred_element_type=F32)
        acc_ref[q_rows, :] = _lane_tile(alpha, 2) * acc_ref[q_rows, :] + pv
        m_ref[q_rows, :] = m_new

    blocks_per_trip = tq // tk

    def body(j, carry):
        for u in range(blocks_per_trip):
            k_rows = pl.ds(pl.multiple_of((j * blocks_per_trip + u) * tk, tk), tk)
            for g in range(tq // rg):
                chain(g, k_rows, tk, None)
        return carry

    lax.fori_loop(0, i, body, 0)
    base = pl.multiple_of(i * tq, tq)
    for g in range(tq // rg):
        n_k = (g + 1) * rg
        chain(g, pl.ds(base, n_k), n_k, g * rg)
    acc = acc_ref[...]
    o_ref[...] = (acc[:, :HEAD_DIM] / acc[:, HEAD_DIM:]).astype(o_ref.dtype)


def _flash(qm, qe, km, ke, v, *, B, S, H, tq, tk, qm_off=0, km_off=0, v_off=0, ke_shared=False):
    T = B * S
    nq = S // tq
    qspec = lambda off: pl.BlockSpec((tq, HEAD_DIM), lambda b, h, i: (b * nq + i, off + h))
    kspec = lambda off: pl.BlockSpec((S, HEAD_DIM), lambda b, h, i: (b, off + h))
    ke_spec = pl.BlockSpec((S, HEAD_DIM), (lambda b, h, i: (b, 0)) if ke_shared else (lambda b, h, i: (b, h)))
    return pl.pallas_call(
        functools.partial(_flash_kernel, tq=tq, tk=tk),
        grid=(B, H, nq),
        in_specs=[qspec(qm_off), qspec(0), kspec(km_off), ke_spec, kspec(v_off)],
        out_specs=pl.BlockSpec((tq, HEAD_DIM), lambda b, h, i: (b * nq + i, h)),
        out_shape=jax.ShapeDtypeStruct((T, H * HEAD_DIM), BF16),
        scratch_shapes=[pltpu.VMEM((tq, LANES), F32), pltpu.VMEM((tq, 2 * HEAD_DIM), F32)],
        compiler_params=_cparams(3),
        name="causal_attention",
    )(qm, qe, km, ke, v)


def _dwconv_kernel(a_ref, b_ref, ba_ref, bb_ref, w_ref, wb_ref, o_ref, u_ref, halo_ref, *, tm, tiles_per_seq):
    c = pl.program_id(1)

    @pl.when(pl.program_id(0) % tiles_per_seq == 0)
    def _():
        halo_ref[c] = jnp.zeros((CONV_HALO, LANES), F32)

    u = (a_ref[...].astype(F32) + ba_ref[...]) * jax.nn.sigmoid(b_ref[...].astype(F32) + bb_ref[...])
    u_ref[:CONV_HALO, :] = halo_ref[c]
    u_ref[CONV_HALO:, :] = u
    halo_ref[c] = u[tm - CONV_HALO:, :]
    acc = jnp.zeros((tm, LANES), F32) + wb_ref[...]
    for j in range(CONV_K):
        start = CONV_HALO - (CONV_K - 1) + j
        acc = acc + u_ref[start:start + tm, :] * w_ref[j:j + 1, :]
    o_ref[...] = acc


def _dwconv(proj, glu_bias_a, glu_bias_b, w_dw, b_dw, *, S, GW, tm, off_a, off_b):
    T = proj.shape[0]
    nc = GW // LANES
    row = pl.BlockSpec((1, LANES), lambda i, c: (0, c))
    return pl.pallas_call(
        functools.partial(_dwconv_kernel, tm=tm, tiles_per_seq=S // tm),
        grid=(T // tm, nc),
        in_specs=[
            pl.BlockSpec((tm, LANES), lambda i, c: (i, off_a // LANES + c)),
            pl.BlockSpec((tm, LANES), lambda i, c: (i, off_b // LANES + c)),
            row, row,
            pl.BlockSpec((CONV_HALO, LANES), lambda i, c: (0, c)),
            row,
        ],
        out_specs=pl.BlockSpec((tm, LANES), lambda i, c: (i, c)),
        out_shape=jax.ShapeDtypeStruct((T, GW), F32),
        scratch_shapes=[pltpu.VMEM((tm + CONV_HALO, LANES), F32), pltpu.VMEM((nc, CONV_HALO, LANES), F32)],
        compiler_params=_cparams(2),
        name="glu_dwconv",
    )(proj, proj, glu_bias_a, glu_bias_b, w_dw, b_dw)


def _conv_pw_kernel(y_ref, g_ref, b_ref, w_ref, pb_ref, o_ref):
    y = y_ref[...]
    mu = jnp.mean(y, axis=-1, keepdims=True)
    yc = y - mu
    var = jnp.mean(yc * yc, axis=-1, keepdims=True)
    z = _silu(yc * lax.rsqrt(var + LN_EPS) * g_ref[...] + b_ref[...]).astype(BF16)
    o_ref[...] = (jnp.dot(z, w_ref[...], preferred_element_type=F32) + pb_ref[...]).astype(o_ref.dtype)


def _conv_pw(y, gain, bias, w_pw, pw_bias, *, tm):
    T, GW = y.shape
    row = pl.BlockSpec((1, GW), lambda i: (0, 0))
    blk = pl.BlockSpec((tm, GW), lambda i: (i, 0))
    return pl.pallas_call(
        _conv_pw_kernel,
        grid=(T // tm,),
        in_specs=[blk, row, row, pl.BlockSpec((GW, GW), lambda i: (0, 0)), row],
        out_specs=blk,
        out_shape=jax.ShapeDtypeStruct((T, GW), BF16),
        compiler_params=_cparams(1),
        name="conv_ln_pointwise",
    )(y, gain, bias, w_pw, pw_bias)


def _park_tile(o_ref, ss_ref, z, j, tn):
    o_ref[:, pl.ds(pl.multiple_of(j * tn, tn), tn)] = z
    ss_ref[...] += jnp.sum(z * z, axis=-1, keepdims=True)


def _finish_rows(h_ref, gain_ref, o_ref, ss_ref):
    d_model = o_ref.shape[1]

    def chunk(rows):
        r = lax.rsqrt(ss_ref[rows, :] * (1.0 / d_model) + NORM_EPS)
        _add_scaled_rows(h_ref, gain_ref, o_ref, rows, r)

    _for_row_chunks(o_ref.shape[0], chunk)


def _out_proj_kernel(y0_ref, y1_ref, y2_ref, y3_ref, w_ref, h_ref, g_ref, o_ref, ss_ref,
                     *, gw, tn, n_tiles):
    j = pl.program_id(1)

    @pl.when(j == 0)
    def _():
        ss_ref[...] = jnp.zeros_like(ss_ref)

    z = jnp.dot(y0_ref[...], w_ref[0:gw, :], preferred_element_type=F32)
    z += jnp.dot(y1_ref[...], w_ref[gw:2 * gw, :], preferred_element_type=F32)
    z += jnp.dot(y2_ref[...], w_ref[2 * gw:3 * gw, :], preferred_element_type=F32)
    z += jnp.dot(y3_ref[...], w_ref[3 * gw:4 * gw, :], preferred_element_type=F32)
    _park_tile(o_ref, ss_ref, z, j, tn)

    @pl.when(j == n_tiles - 1)
    def _():
        _finish_rows(h_ref, g_ref, o_ref, ss_ref)


def _out_proj(ys, w, h, g, *, tm, tn):
    T, D = h.shape
    GW = ys[0].shape[1]
    n_tiles = D // tn
    yspec = pl.BlockSpec((tm, GW), lambda i, j: (i, 0))
    return pl.pallas_call(
        functools.partial(_out_proj_kernel, gw=GW, tn=tn, n_tiles=n_tiles),
        grid=(T // tm, n_tiles),
        in_specs=[yspec, yspec, yspec, yspec,
                  pl.BlockSpec((N_GROUPS * GW, tn), lambda i, j: (0, j)),
                  pl.BlockSpec((tm, D), lambda i, j: (i, 0)),
                  pl.BlockSpec((1, D), lambda i, j: (0, 0))],
        out_specs=pl.BlockSpec((tm, D), lambda i, j: (i, 0)),
        out_shape=jax.ShapeDtypeStruct((T, D), F32),
        scratch_shapes=[pltpu.VMEM((tm, 1), F32)],
        compiler_params=_cparams(2),
        name="mix_out_proj",
    )(*ys, w, h, g)


def _ple_kernel(h_ref, gin_ref, wg_ref, p_ref, wp_ref, gpost_ref, o_ref, xn_ref, ss_ref, *, tn, n_tiles):
    j = pl.program_id(1)

    @pl.when(j == 0)
    def _():
        _rms_to_bf16(h_ref, gin_ref, xn_ref)
        ss_ref[...] = jnp.zeros_like(ss_ref)

    gate = jax.nn.sigmoid(jnp.dot(xn_ref[...], wg_ref[...], preferred_element_type=F32))
    z = gate * jnp.dot(p_ref[...], wp_ref[...], preferred_element_type=F32)
    _park_tile(o_ref, ss_ref, z, j, tn)

    @pl.when(j == n_tiles - 1)
    def _():
        _finish_rows(h_ref, gpost_ref, o_ref, ss_ref)


def _ple(h, g_in, wg, p, wp, g_post, *, tm, tn):
    T, D = h.shape
    P = p.shape[1]
    n_tiles = D // tn
    return pl.pallas_call(
        functools.partial(_ple_kernel, tn=tn, n_tiles=n_tiles),
        grid=(T // tm, n_tiles),
        in_specs=[pl.BlockSpec((tm, D), lambda i, j: (i, 0)),
                  pl.BlockSpec((1, D), lambda i, j: (0, 0)),
                  pl.BlockSpec((D, tn), lambda i, j: (0, j)),
                  pl.BlockSpec((tm, P), lambda i, j: (i, 0)),
                  pl.BlockSpec((P, tn), lambda i, j: (0, j)),
                  pl.BlockSpec((1, D), lambda i, j: (0, 0))],
        out_specs=pl.BlockSpec((tm, D), lambda i, j: (i, 0)),
        out_shape=jax.ShapeDtypeStruct((T, D), F32),
        scratch_shapes=[pltpu.VMEM((tm, D), BF16), pltpu.VMEM((tm, 1), F32)],
        compiler_params=_cparams(2),
        name="ple",
    )(h, g_in, wg, p, wp, g_post)


def _rope_lane_layout(w, half):
    z = jnp.zeros((w.shape[0], LANES // 2 - half), w.dtype)
    return jnp.concatenate([w[:, :half], z, w[:, half:], z], axis=1)


def _pad_cols(w, n):
    return jnp.pad(w, ((0, 0), (0, n - w.shape[1])))


def _plan(D, F, S, T):
    return dict(
        tm=_tile(T if T < S else S, 512),
        tf=_tile(F, 256),
        tn=_tile(D, 1024),
        tn_proj=1024,
        tq=_tile(S, 2048),
        tk=_tile(S, 512),
        c_ret=_tile(S, 256),
    )


def kernel(x, p, positions, ffn1_norm_pre, ffn1_w_gate, ffn1_w_up, ffn1_w_down, ffn1_norm_post, mix_norm_pre, w_in, mla_q_norm, mla_w_uq, mla_kv_norm, mla_w_ukv, ret_gn_gain, ret_gn_bias, fox_forget_bias, conv_glu_bias, conv_dw, conv_dw_bias, conv_ln_gain, conv_ln_bias, conv_w_pw, conv_pw_bias, w_out, mix_norm_post, ffn2_norm_pre, ffn2_w_gate, ffn2_w_up, ffn2_w_down, ffn2_norm_post, ple_norm_in, ple_w_gate, ple_w_proj, ple_norm_post):
    B, S, D = x.shape
    depth = p.shape[0]
    T = B * S
    F = ffn1_w_gate.shape[2]
    GW = D // N_GROUPS
    H = GW // HEAD_DIM
    KVL = mla_kv_norm.shape[1]
    NOPE = HEAD_DIM
    plan = _plan(D, F, S, T)
    tm, tf, tn, tq, tk, c_ret = (plan[n] for n in ("tm", "tf", "tn", "tq", "tk", "c_ret"))
    row = lambda v: v.reshape(1, -1).astype(F32)

    h = x.reshape(T, D)
    pos_col = positions.reshape(T, 1).astype(F32)
    cos_r, sin_r, cos_m, sin_m = _rope_tables(pos_col, tm=tm)

    sp = np.cumsum([GW] * 4 + [GW, KVL, MLA_ROPE] + [GW] * 3 + [H, 2 * GW])
    off_cq, off_ckv, off_kr = 4 * GW, 5 * GW, 5 * GW + KVL
    off_fox = off_kr + LANES
    off_conv = off_fox + 3 * GW
    n_main = off_conv + 2 * GW
    tn_proj = plan["tn_proj"]
    n_main_pad = -(-n_main // tn_proj) * tn_proj
    assert mla_q_norm.shape[1] == GW and off_ckv % KVL == 0 and KVL % LANES == 0
    mla_scale = (NOPE + MLA_ROPE) ** -0.5 * LOG2E
    col_scale = np.ones((int(sp[-1]),), np.float32)
    col_scale[sp[0]:sp[1]] = HEAD_DIM ** -0.5
    col_scale[sp[6]:sp[7]] = HEAD_DIM ** -0.5 * LOG2E

    for i in range(depth):
        w_main, w_f = _mix_in_weights(w_in, jnp.asarray(col_scale[None, :]), i, sp=tuple(int(v) for v in sp),
                                      off_kr=off_kr, n_main=n_main, n_pad=n_main_pad)
        uq = mla_w_uq[i].reshape(GW, H, NOPE + MLA_ROPE) * mla_scale
        uq_nope = uq[:, :, :NOPE].reshape(GW, H * NOPE)
        uq_rope = jnp.concatenate(
            [_rope_lane_layout(uq[:, hh, NOPE:], MLA_ROPE // 2) for hh in range(H)], axis=1)
        w_uq = jnp.concatenate([uq_nope, uq_rope], axis=1).astype(BF16)
        ukv = mla_w_ukv[i].reshape(KVL, H, 2 * HEAD_DIM)
        w_ukv = jnp.concatenate([ukv[:, :, :NOPE].reshape(KVL, GW), ukv[:, :, NOPE:].reshape(KVL, GW)],
                                axis=1).astype(BF16)
        ffn_w = [tuple(_to_bf16(w_, i) for w_ in ws)
                 for ws in ((ffn1_w_gate, ffn1_w_up, ffn1_w_down), (ffn2_w_gate, ffn2_w_up, ffn2_w_down))]
        dw = jnp.pad(conv_dw[i], ((0, CONV_HALO - CONV_K), (0, 0)))

        h = _ffn(h, row(ffn1_norm_pre[i]), *ffn_w[0], row(ffn1_norm_post[i]), tm=tm, tf=tf)

        proj, logits = _proj(h, row(mix_norm_pre[i]), w_main, w_f, tm=tm, tn=tn_proj)
        y_ret = _retention(proj, cos_r, sin_r, row(ret_gn_gain[i]), row(ret_gn_bias[i]), B=B, S=S, GW=GW, C=c_ret)
        qn, qr, kn, mv, kro = _mla_prep(proj, row(mla_q_norm[i]), row(mla_kv_norm[i]), w_uq, w_ukv, cos_m, sin_m,
                                        GW=GW, KVL=KVL, tm=tm, off_cq=off_cq, off_ckv=off_ckv, off_kr=off_kr)
        y_mla = _flash(qn, qr, kn, kro, mv, B=B, S=S, H=H, tq=tq, tk=tk, ke_shared=True)
        fbias = jnp.pad(fox_forget_bias[i], (0, LANES - H)).reshape(1, LANES)
        qe, ke = _fox_prep(logits, fbias, S=S, GW=GW, tm=tm)
        hb = off_fox // HEAD_DIM
        y_fox = _flash(proj, qe, proj, ke, proj, B=B, S=S, H=H, tq=tq, tk=tk,
                       qm_off=hb, km_off=hb + H, v_off=hb + 2 * H)
        gb = conv_glu_bias[i]
        y_dw = _dwconv(proj, row(gb[:GW]), row(gb[GW:]), dw, row(conv_dw_bias[i]),
                       S=S, GW=GW, tm=tm, off_a=off_conv, off_b=off_conv + GW)
        y_conv = _conv_pw(y_dw, row(conv_ln_gain[i]), row(conv_ln_bias[i]), conv_w_pw[i].astype(BF16),
                          row(conv_pw_bias[i]), tm=tm)
        h = _out_proj([y_ret, y_mla, y_fox, y_conv], _to_bf16(w_out, i), h, row(mix_norm_post[i]), tm=tm, tn=tn)

        h = _ffn(h, row(ffn2_norm_pre[i]), *ffn_w[1], row(ffn2_norm_post[i]), tm=tm, tf=tf)

        h = _ple(h, row(ple_norm_in[i]), _to_bf16(ple_w_gate, i), p[i].reshape(T, -1).astype(BF16),
                 ple_w_proj[i].astype(BF16), row(ple_norm_post[i]), tm=tm, tn=tn)

    return h.reshape(B, S, D)
```

```python
import functools

import numpy as np
import jax
import jax.numpy as jnp
from jax import lax
from jax.experimental import pallas as pl
from jax.experimental.pallas import tpu as pltpu

F32 = jnp.float32
BF16 = jnp.bfloat16

HEAD_DIM = 128
LANES = 128
N_GROUPS = 4
MLA_ROPE = 64
CONV_K = 31
CONV_HALO = 32
ROPE_THETA = 10000.0
NORM_EPS = 1e-6
LN_EPS = 1e-5
NEG_BIG = -1e30
VMEM_LIMIT = 60 * 1024 * 1024


def _cparams(n_axes):
    return pltpu.CompilerParams(dimension_semantics=("arbitrary",) * n_axes,
                                vmem_limit_bytes=VMEM_LIMIT)


def _tile(n, pref):
    if n <= pref:
        return n
    t = pref - pref % LANES
    while t >= LANES:
        if n % t == 0:
            return t
        t -= LANES
    raise ValueError(f"no lane-aligned tile for {n}")


def _rms(x, gain, eps=NORM_EPS):
    return x * lax.rsqrt(jnp.mean(x * x, axis=-1, keepdims=True) + eps) * gain


def _silu(x):
    return x * jax.nn.sigmoid(x)


ROW_CHUNK = 64
COL_CHUNK = 512


def _for_row_chunks(n_rows, fn):
    chunk = min(ROW_CHUNK, n_rows)

    def body(r, carry):
        fn(pl.ds(pl.multiple_of(r * chunk, chunk), chunk))
        return carry

    lax.fori_loop(0, n_rows // chunk, body, 0)


def _col_slices(n_cols):
    cc = min(COL_CHUNK, n_cols)
    return [slice(c * cc, (c + 1) * cc) for c in range(n_cols // cc)]


def _row_inv_rms(src_ref, rows):
    n_cols = src_ref.shape[1]
    ss = None
    for sl in _col_slices(n_cols):
        x = src_ref[rows, sl]
        part = jnp.sum(x * x, axis=-1, keepdims=True)
        ss = part if ss is None else ss + part
    return lax.rsqrt(ss * (1.0 / n_cols) + NORM_EPS)


def _rms_to_bf16(src_ref, gain_ref, dst_ref):
    def chunk(rows):
        r = _row_inv_rms(src_ref, rows)
        for sl in _col_slices(src_ref.shape[1]):
            dst_ref[rows, sl] = (src_ref[rows, sl] * r * gain_ref[:, sl]).astype(BF16)

    _for_row_chunks(src_ref.shape[0], chunk)


def _add_scaled_rows(h_ref, gain_ref, o_ref, rows, r):
    for sl in _col_slices(o_ref.shape[1]):
        o_ref[rows, sl] = h_ref[rows, sl] + o_ref[rows, sl] * r * gain_ref[:, sl]


CAST_BLOCK_BYTES = 8 * 1024 * 1024
BF16_SUBLANES = 16


def _cast_kernel(x_ref, o_ref):
    o_ref[...] = x_ref[...].astype(BF16)


def _to_bf16(w_stack, layer):
    _, R, C = w_stack.shape
    rows = R
    for cand in range(BF16_SUBLANES, R + 1, BF16_SUBLANES):
        if R % cand == 0 and cand * C * 4 <= CAST_BLOCK_BYTES:
            rows = cand
    if rows == R and R * C * 4 > CAST_BLOCK_BYTES:
        return w_stack[layer].astype(BF16)
    return pl.pallas_call(
        _cast_kernel,
        grid=(R // rows,),
        in_specs=[pl.BlockSpec((None, rows, C), lambda r: (layer, r, 0))],
        out_specs=pl.BlockSpec((rows, C), lambda r: (r, 0)),
        out_shape=jax.ShapeDtypeStruct((R, C), BF16),
        compiler_params=_cparams(1),
        name="cast_bf16",
    )(w_stack)


W_IN_ROWS = 128


def _mix_in_weights_kernel(x_ref, s_ref, o_ref, *, sp, off_kr, n_main):
    rows = x_ref.shape[0]
    half = MLA_ROPE // 2

    def piece(a, b):
        return (x_ref[:, a:b] * s_ref[:, a:b]).astype(BF16)

    o_ref[:, :sp[5]] = piece(0, sp[5])
    kr = piece(sp[5], sp[6])
    gap = jnp.zeros((rows, LANES // 2 - half), BF16)
    o_ref[:, off_kr:off_kr + LANES] = jnp.concatenate([kr[:, :half], gap, kr[:, half:], gap], axis=1)
    off_fox = off_kr + LANES
    off_conv = off_fox + sp[9] - sp[6]
    o_ref[:, off_fox:off_conv] = piece(sp[6], sp[9])
    o_ref[:, off_conv:n_main] = piece(sp[10], sp[11])
    n_f = sp[10] - sp[9]
    o_ref[:, n_main:] = jnp.concatenate(
        [piece(sp[9], sp[10]), jnp.zeros((rows, o_ref.shape[1] - n_main - n_f), BF16)], axis=1)


def _mix_in_weights(w_in, col_scale, layer, *, sp, off_kr, n_main, n_pad):
    _, D, n_in = w_in.shape
    rows = min(W_IN_ROWS, D)
    return pl.pallas_call(
        functools.partial(_mix_in_weights_kernel, sp=sp, off_kr=off_kr, n_main=n_main),
        grid=(D // rows,),
        in_specs=[pl.BlockSpec((None, rows, n_in), lambda r: (layer, r, 0)),
                  pl.BlockSpec((1, n_in), lambda r: (0, 0))],
        out_specs=pl.BlockSpec((rows, n_pad), lambda r: (r, 0)),
        out_shape=jax.ShapeDtypeStruct((D, n_pad), BF16),
        compiler_params=_cparams(1),
        name="mix_in_weights",
    )(w_in, col_scale)


def _ffn_kernel(h_ref, gpre_ref, wg_ref, wu_ref, wd_ref, gpost_ref, o_ref, xn_ref):
    j = pl.program_id(1)

    @pl.when(j == 0)
    def _():
        _rms_to_bf16(h_ref, gpre_ref, xn_ref)
        o_ref[...] = jnp.zeros_like(o_ref)

    xn = xn_ref[...]
    g = jnp.dot(xn, wg_ref[...], preferred_element_type=F32)
    u = jnp.dot(xn, wu_ref[...], preferred_element_type=F32)
    a = (_silu(g) * u).astype(BF16)
    d_model = o_ref.shape[1]
    cc = min(COL_CHUNK, d_model)
    for n in range(d_model // cc):
        sl = slice(n * cc, (n + 1) * cc)
        o_ref[:, sl] += jnp.dot(a, wd_ref[:, sl], preferred_element_type=F32)

    @pl.when(j == pl.num_programs(1) - 1)
    def _():
        def chunk(rows):
            _add_scaled_rows(h_ref, gpost_ref, o_ref, rows, 0.5 * _row_inv_rms(o_ref, rows))

        _for_row_chunks(o_ref.shape[0], chunk)


def _ffn(h, g_pre, wg, wu, wd, g_post, *, tm, tf):
    T, D = h.shape
    Fp = wg.shape[1]
    return pl.pallas_call(
        _ffn_kernel,
        grid=(T // tm, Fp // tf),
        in_specs=[
            pl.BlockSpec((tm, D), lambda i, j: (i, 0)),
            pl.BlockSpec((1, D), lambda i, j: (0, 0)),
            pl.BlockSpec((D, tf), lambda i, j: (0, j)),
            pl.BlockSpec((D, tf), lambda i, j: (0, j)),
            pl.BlockSpec((tf, D), lambda i, j: (j, 0)),
            pl.BlockSpec((1, D), lambda i, j: (0, 0)),
        ],
        out_specs=pl.BlockSpec((tm, D), lambda i, j: (i, 0)),
        out_shape=jax.ShapeDtypeStruct((T, D), F32),
        scratch_shapes=[pltpu.VMEM((tm, D), BF16)],
        compiler_params=_cparams(2),
        name="ffn",
    )(h, g_pre, wg, wu, wd, g_post)


def _proj_kernel(h_ref, g_ref, w_ref, o_ref, of_ref, xn_ref, *, f32_lanes):
    @pl.when(pl.program_id(1) == 0)
    def _():
        _rms_to_bf16(h_ref, g_ref, xn_ref)

    z = jnp.dot(xn_ref[...], w_ref[...], preferred_element_type=F32)
    o_ref[...] = z.astype(o_ref.dtype)

    @pl.when(pl.program_id(1) == pl.num_programs(1) - 1)
    def _():
        of_ref[...] = z[:, f32_lanes:f32_lanes + LANES]


def _proj(h, g, w, *, tm, tn, f32_col):
    T, D = h.shape
    N = w.shape[1]
    assert f32_col // tn == N // tn - 1 and f32_col % LANES == 0
    return pl.pallas_call(
        functools.partial(_proj_kernel, f32_lanes=f32_col % tn),
        grid=(T // tm, N // tn),
        in_specs=[
            pl.BlockSpec((tm, D), lambda i, j: (i, 0)),
            pl.BlockSpec((1, D), lambda i, j: (0, 0)),
            pl.BlockSpec((D, tn), lambda i, j: (0, j)),
        ],
        out_specs=[
            pl.BlockSpec((tm, tn), lambda i, j: (i, j)),
            pl.BlockSpec((tm, LANES), lambda i, j: (i, 0)),
        ],
        out_shape=[jax.ShapeDtypeStruct((T, N), BF16), jax.ShapeDtypeStruct((T, LANES), F32)],
        scratch_shapes=[pltpu.VMEM((tm, D), BF16)],
        compiler_params=_cparams(2),
        name="mix_in_proj",
    )(h, g, w)


def _rope_kernel(pos_ref, fr_ref, fm_ref, sr_ref, mc_ref, ms_ref, cr_ref, snr_ref, cm_ref, snm_ref):
    pos = pos_ref[...]
    ang_r = pos * fr_ref[...]
    cr_ref[...] = jnp.cos(ang_r)
    snr_ref[...] = jnp.sin(ang_r) * sr_ref[...]
    ang_m = pos * fm_ref[...]
    cm_ref[...] = jnp.cos(ang_m) * mc_ref[...]
    snm_ref[...] = jnp.sin(ang_m) * ms_ref[...]


def _rope_tables(pos_col, *, tm):
    T = pos_col.shape[0]
    lane = np.arange(LANES)
    half_r = HEAD_DIM // 2
    half_m = MLA_ROPE // 2
    freq_r = (ROPE_THETA ** (-(lane % half_r).astype(np.float32) / half_r)).astype(np.float32)
    sign_r = np.where(lane < half_r, -1.0, 1.0).astype(np.float32)
    active = (lane % 64) < half_m
    freq_m = (ROPE_THETA ** (-(lane % 64 % half_m).astype(np.float32) / half_m)).astype(np.float32)
    mask_c = active.astype(np.float32)
    mask_s = np.where(active, np.where(lane < 64, -1.0, 1.0), 0.0).astype(np.float32)
    consts = [jnp.asarray(c[None, :]) for c in (freq_r, freq_m, sign_r, mask_c, mask_s)]
    row = pl.BlockSpec((1, LANES), lambda i: (0, 0))
    tab = pl.BlockSpec((tm, LANES), lambda i: (i, 0))
    return pl.pallas_call(
        _rope_kernel,
        grid=(T // tm,),
        in_specs=[pl.BlockSpec((tm, 1), lambda i: (i, 0)), row, row, row, row, row],
        out_specs=[tab, tab, tab, tab],
        out_shape=[jax.ShapeDtypeStruct((T, LANES), F32)] * 4,
        compiler_params=_cparams(1),
        name="rope_tables",
    )(pos_col, *consts)


def _rot(x, cos, sin):
    return x * cos + pltpu.roll(x, LANES // 2, 1) * sin


def _ret_kernel(q_ref, k_ref, v_ref, g_ref, cos_ref, sin_ref, dm_ref, qd_ref, kd_ref, cd_ref,
                gain_ref, bias_ref, o_ref, st_ref, *, n_heads):
    @pl.when(pl.program_id(1) == 0)
    def _():
        st_ref[...] = jnp.zeros_like(st_ref)

    cos = cos_ref[...]
    sin = sin_ref[...]
    for h in range(n_heads):
        sl = slice(h * HEAD_DIM, (h + 1) * HEAD_DIM)
        qr = _rot(q_ref[:, sl].astype(F32), cos, sin)
        kr = _rot(k_ref[:, sl].astype(F32), cos, sin)
        vb = v_ref[:, sl]
        sc = lax.dot_general(qr.astype(BF16), kr.astype(BF16), (((1,), (1,)), ((), ())),
                             preferred_element_type=F32) * dm_ref[h]
        inner = jnp.dot(sc.astype(BF16), vb, preferred_element_type=F32)
        st = st_ref[h]
        cross = jnp.dot((qr * qd_ref[h]).astype(BF16), st.astype(BF16), preferred_element_type=F32)
        kdt = (kr * kd_ref[h]).T.astype(BF16)
        st_ref[h] = st * cd_ref[h] + jnp.dot(kdt, vb, preferred_element_type=F32)
        o = inner + cross
        mu = jnp.mean(o, axis=-1, keepdims=True)
        oc = o - mu
        var = jnp.mean(oc * oc, axis=-1, keepdims=True)
        y = (oc * lax.rsqrt(var + LN_EPS)) * gain_ref[:, sl] + bias_ref[:, sl]
        o_ref[:, sl] = (y * _silu(g_ref[:, sl].astype(F32))).astype(o_ref.dtype)


def _retention(proj, cos_r, sin_r, gain, bias, *, B, S, GW, C):
    T = B * S
    H = GW // HEAD_DIM
    nC = S // C
    idx = np.arange(C, dtype=np.float64)
    log_gamma = np.log(1.0 - 2.0 ** (-5.0 - np.arange(H, dtype=np.float64)))[:, None, None]
    diff = idx[:, None] - idx[None, :]
    dmat = np.where(diff >= 0, np.exp(log_gamma * np.maximum(diff, 0.0)), 0.0)
    qdec = np.broadcast_to(np.exp(log_gamma * (idx + 1.0)[None, :, None]), (H, C, LANES))
    kdec = np.broadcast_to(np.exp(log_gamma * (C - 1.0 - idx)[None, :, None]), (H, C, LANES))
    cdec = np.broadcast_to(np.exp(log_gamma * C), (H, 1, LANES))
    consts = [jnp.asarray(np.ascontiguousarray(c), F32) for c in (dmat, qdec, kdec, cdec)]

    def col(c):
        return pl.BlockSpec((C, GW), lambda b, i, c=c: (b * nC + i, c))

    def const(shape):
        return pl.BlockSpec(shape, lambda b, i: (0,) * len(shape))

    tab = pl.BlockSpec((C, LANES), lambda b, i: (b * nC + i, 0))
    return pl.pallas_call(
        functools.partial(_ret_kernel, n_heads=H),
        grid=(B, nC),
        in_specs=[col(0), col(1), col(2), col(3), tab, tab,
                  const((H, C, C)), const((H, C, LANES)), const((H, C, LANES)), const((H, 1, LANES)),
                  const((1, GW)), const((1, GW))],
        out_specs=pl.BlockSpec((C, GW), lambda b, i: (b * nC + i, 0)),
        out_shape=jax.ShapeDtypeStruct((T, GW), BF16),
        scratch_shapes=[pltpu.VMEM((H, HEAD_DIM, HEAD_DIM), F32)],
        compiler_params=_cparams(2),
        name="retention",
    )(proj, proj, proj, proj, cos_r, sin_r, *consts, gain, bias)


def _mla_prep_kernel(cq_ref, ckv_ref, kr_ref, gq_ref, gkv_ref, wq_ref, wkv_ref, cos_ref, sin_ref,
                     qn_ref, qr_ref, kn_ref, v_ref, kro_ref, *, gw):
    cos = cos_ref[...]
    sin = sin_ref[...]
    cqn = _rms(cq_ref[...].astype(F32), gq_ref[...]).astype(BF16)
    q = jnp.dot(cqn, wq_ref[...], preferred_element_type=F32)
    qn_ref[...] = q[:, :gw].astype(BF16)
    for h in range(gw // HEAD_DIM):
        sl = slice(h * HEAD_DIM, (h + 1) * HEAD_DIM)
        qr_ref[:, sl] = _rot(q[:, gw + h * HEAD_DIM:gw + (h + 1) * HEAD_DIM], cos, sin).astype(BF16)
    ckn = _rms(ckv_ref[...].astype(F32), gkv_ref[...]).astype(BF16)
    kv = jnp.dot(ckn, wkv_ref[...], preferred_element_type=F32)
    kn_ref[...] = kv[:, :gw].astype(BF16)
    v_ref[...] = kv[:, gw:].astype(BF16)
    kro_ref[...] = _rot(kr_ref[...].astype(F32), cos, sin).astype(BF16)


def _mla_prep(proj, gq, gkv, wq, wkv, cos_m, sin_m, *, GW, KVL, tm, off_cq, off_ckv, off_kr):
    T = proj.shape[0]
    row = lambda n: pl.BlockSpec((1, n), lambda i: (0, 0))
    blk = lambda n: pl.BlockSpec((tm, n), lambda i: (i, 0))
    return pl.pallas_call(
        functools.partial(_mla_prep_kernel, gw=GW),
        grid=(T // tm,),
        in_specs=[
            pl.BlockSpec((tm, GW), lambda i: (i, off_cq // GW)),
            pl.BlockSpec((tm, KVL), lambda i: (i, off_ckv // KVL)),
            pl.BlockSpec((tm, LANES), lambda i: (i, off_kr // LANES)),
            row(GW), row(KVL),
            pl.BlockSpec((GW, 2 * GW), lambda i: (0, 0)),
            pl.BlockSpec((KVL, 2 * GW), lambda i: (0, 0)),
            blk(LANES), blk(LANES),
        ],
        out_specs=[blk(GW), blk(GW), blk(GW), blk(GW), blk(LANES)],
        out_shape=[jax.ShapeDtypeStruct((T, GW), BF16)] * 4 + [jax.ShapeDtypeStruct((T, LANES), BF16)],
        compiler_params=_cparams(1),
        name="mla_prep",
    )(proj, proj, proj, gq, gkv, wq, wkv, cos_m, sin_m)


def _split3(x):
    hi = x.astype(BF16)
    r1 = x - hi.astype(F32)
    mid = r1.astype(BF16)
    lo = (r1 - mid.astype(F32)).astype(BF16)
    return hi, mid, lo


def _fox_prep_kernel(lg_ref, fb_ref, tri_ref, selq_ref, selk_ref, oneq_ref, onek_ref,
                     qe_ref, ke_ref, carry_ref, *, tiles_per_seq):
    @pl.when(pl.program_id(0) % tiles_per_seq == 0)
    def _():
        carry_ref[...] = jnp.zeros_like(carry_ref)

    x = lg_ref[...] + fb_ref[...]
    logf = jnp.minimum(x, 0.0) - jnp.log(1.0 + jnp.exp(-jnp.abs(x)))
    tri = tri_ref[...]
    cum = carry_ref[...]
    for part in _split3(logf):
        cum = cum + jnp.dot(tri, part, preferred_element_type=F32)
    carry_ref[...] = cum[cum.shape[0] - 1:, :]
    parts = jnp.concatenate(_split3(cum * LOG2E), axis=-1)
    qe_ref[...] = (jnp.dot(parts, selq_ref[...], preferred_element_type=F32) + oneq_ref[...]).astype(BF16)
    ke_ref[...] = (jnp.dot(parts, selk_ref[...], preferred_element_type=F32) + onek_ref[...]).astype(BF16)


def _fox_prep(logits, fbias, *, S, GW, tm):
    T = logits.shape[0]
    H = GW // HEAD_DIM
    tri = np.tril(np.ones((tm, tm), np.float32))
    selq = np.zeros((3 * LANES, GW), np.float32)
    selk = np.zeros((3 * LANES, GW), np.float32)
    oneq = np.zeros((1, GW), np.float32)
    onek = np.zeros((1, GW), np.float32)
    for h in range(H):
        for part in range(3):
            selq[part * LANES + h, h * HEAD_DIM + part] = 1.0
            selk[part * LANES + h, h * HEAD_DIM + 3 + part] = -1.0
            oneq[0, h * HEAD_DIM + 3 + part] = 1.0
            onek[0, h * HEAD_DIM + part] = 1.0
    const = lambda a: pl.BlockSpec(a.shape, lambda i: (0, 0))
    consts = [jnp.asarray(tri, BF16), jnp.asarray(selq, BF16), jnp.asarray(selk, BF16),
              jnp.asarray(oneq), jnp.asarray(onek)]
    return pl.pallas_call(
        functools.partial(_fox_prep_kernel, tiles_per_seq=S // tm),
        grid=(T // tm,),
        in_specs=[pl.BlockSpec((tm, LANES), lambda i: (i, 0)), pl.BlockSpec((1, LANES), lambda i: (0, 0))]
                 + [const(c) for c in consts],
        out_specs=[pl.BlockSpec((tm, GW), lambda i: (i, 0))] * 2,
        out_shape=[jax.ShapeDtypeStruct((T, GW), BF16)] * 2,
        scratch_shapes=[pltpu.VMEM((1, LANES), F32)],
        compiler_params=_cparams(1),
        name="fox_prep",
    )(logits, fbias, *consts)


ATTN_ROWS = 256
LOG2E = 1.4426950408889634


def _lane_tile(x, n):
    return x if n == 1 else jnp.concatenate([x] * n, axis=1)


def _flash_kernel(qm_ref, qe_ref, km_ref, ke_ref, v_ref, o_ref, m_ref, acc_ref, *, tq, tk):
    i = pl.program_id(2)
    rg = min(ATTN_ROWS, tq)
    m_ref[...] = jnp.full_like(m_ref, NEG_BIG)
    acc_ref[...] = jnp.zeros_like(acc_ref)

    def chain(g, k_rows, n_k, causal_tail):
        q_rows = slice(g * rg, (g + 1) * rg)
        q = jnp.concatenate([qm_ref[q_rows, :], qe_ref[q_rows, :]], axis=-1)
        k = jnp.concatenate([km_ref[k_rows, :], ke_ref[k_rows, :]], axis=-1)
        v = jnp.concatenate([v_ref[k_rows, :], jnp.ones((n_k, HEAD_DIM), BF16)], axis=-1)
        s = lax.dot_general(q, k, (((1,), (1,)), ((), ())), preferred_element_type=F32)
        if causal_tail:
            r = lax.broadcasted_iota(jnp.int32, (rg, rg), 0)
            c = lax.broadcasted_iota(jnp.int32, (rg, rg), 1)
            tail = jnp.where(c <= r, s[:, n_k - rg:], NEG_BIG)
            s = tail if n_k == rg else jnp.concatenate([s[:, :n_k - rg], tail], axis=1)
        m_prev = m_ref[q_rows, :]
        m_new = jnp.maximum(m_prev, jnp.max(s, axis=-1, keepdims=True))
        alpha = jnp.exp2(m_prev - m_new)
        p = jnp.exp2(s - _lane_tile(m_new, n_k // LANES))
        pv = jnp.dot(p.astype(BF16), v, preferred_element_type=F32)
        acc_ref[q_rows, :] = _lane_tile(alpha, 2) * acc_ref[q_rows, :] + pv
        m_ref[q_rows, :] = m_new

    blocks_per_trip = tq // tk

    def body(j, carry):
        for u in range(blocks_per_trip):
            k_rows = pl.ds(pl.multiple_of((j * blocks_per_trip + u) * tk, tk), tk)
            for g in range(tq // rg):
                chain(g, k_rows, tk, False)
        return carry

    lax.fori_loop(0, i, body, 0)
    base = pl.multiple_of(i * tq, tq)
    for g in reversed(range(tq // rg)):
        n_k = (g + 1) * rg
        chain(g, pl.ds(base, n_k), n_k, True)
    acc = acc_ref[...]
    o_ref[...] = (acc[:, :HEAD_DIM] / acc[:, HEAD_DIM:]).astype(o_ref.dtype)


def _flash(qm, qe, km, ke, v, *, B, S, H, tq, tk, qm_off=0, km_off=0, v_off=0, ke_shared=False):
    T = B * S
    nq = S // tq
    qspec = lambda off: pl.BlockSpec((tq, HEAD_DIM), lambda b, h, i: (b * nq + i, off + h))
    kspec = lambda off: pl.BlockSpec((S, HEAD_DIM), lambda b, h, i: (b, off + h))
    ke_spec = pl.BlockSpec((S, HEAD_DIM), (lambda b, h, i: (b, 0)) if ke_shared else (lambda b, h, i: (b, h)))
    return pl.pallas_call(
        functools.partial(_flash_kernel, tq=tq, tk=tk),
        grid=(B, H, nq),
        in_specs=[qspec(qm_off), qspec(0), kspec(km_off), ke_spec, kspec(v_off)],
        out_specs=pl.BlockSpec((tq, HEAD_DIM), lambda b, h, i: (b * nq + i, h)),
        out_shape=jax.ShapeDtypeStruct((T, H * HEAD_DIM), BF16),
        scratch_shapes=[pltpu.VMEM((tq, LANES), F32), pltpu.VMEM((tq, 2 * HEAD_DIM), F32)],
        compiler_params=_cparams(3),
        name="causal_attention",
    )(qm, qe, km, ke, v)


def _dwconv_kernel(a_ref, b_ref, ba_ref, bb_ref, w_ref, wb_ref, o_ref, u_ref, halo_ref, *, tm, tiles_per_seq):
    c = pl.program_id(1)

    @pl.when(pl.program_id(0) % tiles_per_seq == 0)
    def _():
        halo_ref[c] = jnp.zeros((CONV_HALO, LANES), F32)

    u = (a_ref[...].astype(F32) + ba_ref[...]) * jax.nn.sigmoid(b_ref[...].astype(F32) + bb_ref[...])
    u_ref[:CONV_HALO, :] = halo_ref[c]
    u_ref[CONV_HALO:, :] = u
    halo_ref[c] = u[tm - CONV_HALO:, :]
    acc = jnp.zeros((tm, LANES), F32) + wb_ref[...]
    for j in range(CONV_K):
        start = CONV_HALO - (CONV_K - 1) + j
        acc = acc + u_ref[start:start + tm, :] * w_ref[j:j + 1, :]
    o_ref[...] = acc


def _dwconv(proj, glu_bias_a, glu_bias_b, w_dw, b_dw, *, S, GW, tm, off_a, off_b):
    T = proj.shape[0]
    nc = GW // LANES
    row = pl.BlockSpec((1, LANES), lambda i, c: (0, c))
    return pl.pallas_call(
        functools.partial(_dwconv_kernel, tm=tm, tiles_per_seq=S // tm),
        grid=(T // tm, nc),
        in_specs=[
            pl.BlockSpec((tm, LANES), lambda i, c: (i, off_a // LANES + c)),
            pl.BlockSpec((tm, LANES), lambda i, c: (i, off_b // LANES + c)),
            row, row,
            pl.BlockSpec((CONV_HALO, LANES), lambda i, c: (0, c)),
            row,
        ],
        out_specs=pl.BlockSpec((tm, LANES), lambda i, c: (i, c)),
        out_shape=jax.ShapeDtypeStruct((T, GW), F32),
        scratch_shapes=[pltpu.VMEM((tm + CONV_HALO, LANES), F32), pltpu.VMEM((nc, CONV_HALO, LANES), F32)],
        compiler_params=_cparams(2),
        name="glu_dwconv",
    )(proj, proj, glu_bias_a, glu_bias_b, w_dw, b_dw)


def _conv_pw_kernel(y_ref, g_ref, b_ref, w_ref, pb_ref, o_ref):
    y = y_ref[...]
    mu = jnp.mean(y, axis=-1, keepdims=True)
    yc = y - mu
    var = jnp.mean(yc * yc, axis=-1, keepdims=True)
    z = _silu(yc * lax.rsqrt(var + LN_EPS) * g_ref[...] + b_ref[...]).astype(BF16)
    o_ref[...] = (jnp.dot(z, w_ref[...], preferred_element_type=F32) + pb_ref[...]).astype(o_ref.dtype)


def _conv_pw(y, gain, bias, w_pw, pw_bias, *, tm):
    T, GW = y.shape
    row = pl.BlockSpec((1, GW), lambda i: (0, 0))
    blk = pl.BlockSpec((tm, GW), lambda i: (i, 0))
    return pl.pallas_call(
        _conv_pw_kernel,
        grid=(T // tm,),
        in_specs=[blk, row, row, pl.BlockSpec((GW, GW), lambda i: (0, 0)), row],
        out_specs=blk,
        out_shape=jax.ShapeDtypeStruct((T, GW), BF16),
        compiler_params=_cparams(1),
        name="conv_ln_pointwise",
    )(y, gain, bias, w_pw, pw_bias)


def _park_tile(o_ref, ss_ref, z, j, tn):
    o_ref[:, pl.ds(pl.multiple_of(j * tn, tn), tn)] = z
    ss_ref[...] += jnp.sum(z * z, axis=-1, keepdims=True)


def _finish_rows(h_ref, gain_ref, o_ref, ss_ref):
    d_model = o_ref.shape[1]

    def chunk(rows):
        r = lax.rsqrt(ss_ref[rows, :] * (1.0 / d_model) + NORM_EPS)
        _add_scaled_rows(h_ref, gain_ref, o_ref, rows, r)

    _for_row_chunks(o_ref.shape[0], chunk)


def _out_proj_kernel(y0_ref, y1_ref, y2_ref, y3_ref, w_ref, h_ref, g_ref, o_ref, ss_ref,
                     *, gw, tn, n_tiles):
    j = pl.program_id(1)

    @pl.when(j == 0)
    def _():
        ss_ref[...] = jnp.zeros_like(ss_ref)

    z = jnp.dot(y0_ref[...], w_ref[0:gw, :], preferred_element_type=F32)
    z += jnp.dot(y1_ref[...], w_ref[gw:2 * gw, :], preferred_element_type=F32)
    z += jnp.dot(y2_ref[...], w_ref[2 * gw:3 * gw, :], preferred_element_type=F32)
    z += jnp.dot(y3_ref[...], w_ref[3 * gw:4 * gw, :], preferred_element_type=F32)
    _park_tile(o_ref, ss_ref, z, j, tn)

    @pl.when(j == n_tiles - 1)
    def _():
        _finish_rows(h_ref, g_ref, o_ref, ss_ref)


def _out_proj(ys, w, h, g, *, tm, tn):
    T, D = h.shape
    GW = ys[0].shape[1]
    n_tiles = D // tn
    yspec = pl.BlockSpec((tm, GW), lambda i, j: (i, 0))
    return pl.pallas_call(
        functools.partial(_out_proj_kernel, gw=GW, tn=tn, n_tiles=n_tiles),
        grid=(T // tm, n_tiles),
        in_specs=[yspec, yspec, yspec, yspec,
                  pl.BlockSpec((N_GROUPS * GW, tn), lambda i, j: (0, j)),
                  pl.BlockSpec((tm, D), lambda i, j: (i, 0)),
                  pl.BlockSpec((1, D), lambda i, j: (0, 0))],
        out_specs=pl.BlockSpec((tm, D), lambda i, j: (i, 0)),
        out_shape=jax.ShapeDtypeStruct((T, D), F32),
        scratch_shapes=[pltpu.VMEM((tm, 1), F32)],
        compiler_params=_cparams(2),
        name="mix_out_proj",
    )(*ys, w, h, g)


def _ple_kernel(h_ref, gin_ref, wg_ref, p_ref, wp_ref, gpost_ref, o_ref, xn_ref, ss_ref, *, tn, n_tiles):
    j = pl.program_id(1)

    @pl.when(j == 0)
    def _():
        _rms_to_bf16(h_ref, gin_ref, xn_ref)
        ss_ref[...] = jnp.zeros_like(ss_ref)

    gate = jax.nn.sigmoid(jnp.dot(xn_ref[...], wg_ref[...], preferred_element_type=F32))
    z = gate * jnp.dot(p_ref[...], wp_ref[...], preferred_element_type=F32)
    _park_tile(o_ref, ss_ref, z, j, tn)

    @pl.when(j == n_tiles - 1)
    def _():
        _finish_rows(h_ref, gpost_ref, o_ref, ss_ref)


def _ple(h, g_in, wg, p, wp, g_post, *, tm, tn):
    T, D = h.shape
    P = p.shape[1]
    n_tiles = D // tn
    return pl.pallas_call(
        functools.partial(_ple_kernel, tn=tn, n_tiles=n_tiles),
        grid=(T // tm, n_tiles),
        in_specs=[pl.BlockSpec((tm, D), lambda i, j: (i, 0)),
                  pl.BlockSpec((1, D), lambda i, j: (0, 0)),
                  pl.BlockSpec((D, tn), lambda i, j: (0, j)),
                  pl.BlockSpec((tm, P), lambda i, j: (i, 0)),
                  pl.BlockSpec((P, tn), lambda i, j: (0, j)),
                  pl.BlockSpec((1, D), lambda i, j: (0, 0))],
        out_specs=pl.BlockSpec((tm, D), lambda i, j: (i, 0)),
        out_shape=jax.ShapeDtypeStruct((T, D), F32),
        scratch_shapes=[pltpu.VMEM((tm, D), BF16), pltpu.VMEM((tm, 1), F32)],
        compiler_params=_cparams(2),
        name="ple",
    )(h, g_in, wg, p, wp, g_post)


def _rope_lane_layout(w, half):
    z = jnp.zeros((w.shape[0], LANES // 2 - half), w.dtype)
    return jnp.concatenate([w[:, :half], z, w[:, half:], z], axis=1)


def _plan(D, F, S, T):
    return dict(
        tm=_tile(T if T < S else S, 512),
        tf=_tile(F, 256),
        tn=_tile(D, 1024),
        tn_proj=1024,
        tq=_tile(S, 2048),
        tk=_tile(S, 512),
        c_ret=_tile(S, 256),
    )


def kernel(x, p, positions, ffn1_norm_pre, ffn1_w_gate, ffn1_w_up, ffn1_w_down, ffn1_norm_post, mix_norm_pre, w_in, mla_q_norm, mla_w_uq, mla_kv_norm, mla_w_ukv, ret_gn_gain, ret_gn_bias, fox_forget_bias, conv_glu_bias, conv_dw, conv_dw_bias, conv_ln_gain, conv_ln_bias, conv_w_pw, conv_pw_bias, w_out, mix_norm_post, ffn2_norm_pre, ffn2_w_gate, ffn2_w_up, ffn2_w_down, ffn2_norm_post, ple_norm_in, ple_w_gate, ple_w_proj, ple_norm_post):
    B, S, D = x.shape
    depth = p.shape[0]
    T = B * S
    F = ffn1_w_gate.shape[2]
    GW = D // N_GROUPS
    H = GW // HEAD_DIM
    KVL = mla_kv_norm.shape[1]
    NOPE = HEAD_DIM
    plan = _plan(D, F, S, T)
    tm, tf, tn, tq, tk, c_ret = (plan[n] for n in ("tm", "tf", "tn", "tq", "tk", "c_ret"))
    row = lambda v: v.reshape(1, -1).astype(F32)

    h = x.reshape(T, D)
    pos_col = positions.reshape(T, 1).astype(F32)
    cos_r, sin_r, cos_m, sin_m = _rope_tables(pos_col, tm=tm)

    sp = np.cumsum([GW] * 4 + [GW, KVL, MLA_ROPE] + [GW] * 3 + [H, 2 * GW])
    off_cq, off_ckv, off_kr = 4 * GW, 5 * GW, 5 * GW + KVL
    off_fox = off_kr + LANES
    off_conv = off_fox + 3 * GW
    n_main = off_conv + 2 * GW
    tn_proj = plan["tn_proj"]
    n_main_pad = -(-(n_main + LANES) // tn_proj) * tn_proj
    assert mla_q_norm.shape[1] == GW and off_ckv % KVL == 0 and KVL % LANES == 0
    mla_scale = (NOPE + MLA_ROPE) ** -0.5 * LOG2E
    col_scale = np.ones((int(sp[-1]),), np.float32)
    col_scale[sp[0]:sp[1]] = HEAD_DIM ** -0.5
    col_scale[sp[6]:sp[7]] = HEAD_DIM ** -0.5 * LOG2E

    for i in range(depth):
        w_main = _mix_in_weights(w_in, jnp.asarray(col_scale[None, :]), i, sp=tuple(int(v) for v in sp),
                                 off_kr=off_kr, n_main=n_main, n_pad=n_main_pad)
        uq = mla_w_uq[i].reshape(GW, H, NOPE + MLA_ROPE) * mla_scale
        uq_nope = uq[:, :, :NOPE].reshape(GW, H * NOPE)
        uq_rope = jnp.concatenate(
            [_rope_lane_layout(uq[:, hh, NOPE:], MLA_ROPE // 2) for hh in range(H)], axis=1)
        w_uq = jnp.concatenate([uq_nope, uq_rope], axis=1).astype(BF16)
        ukv = mla_w_ukv[i].reshape(KVL, H, 2 * HEAD_DIM)
        w_ukv = jnp.concatenate([ukv[:, :, :NOPE].reshape(KVL, GW), ukv[:, :, NOPE:].reshape(KVL, GW)],
                                axis=1).astype(BF16)
        ffn_w = [tuple(_to_bf16(w_, i) for w_ in ws)
                 for ws in ((ffn1_w_gate, ffn1_w_up, ffn1_w_down), (ffn2_w_gate, ffn2_w_up, ffn2_w_down))]
        dw = jnp.pad(conv_dw[i], ((0, CONV_HALO - CONV_K), (0, 0)))

        h = _ffn(h, row(ffn1_norm_pre[i]), *ffn_w[0], row(ffn1_norm_post[i]), tm=tm, tf=tf)

        proj, logits = _proj(h, row(mix_norm_pre[i]), w_main, tm=tm, tn=tn_proj, f32_col=n_main)
        y_ret = _retention(proj, cos_r, sin_r, row(ret_gn_gain[i]), row(ret_gn_bias[i]), B=B, S=S, GW=GW, C=c_ret)
        qn, qr, kn, mv, kro = _mla_prep(proj, row(mla_q_norm[i]), row(mla_kv_norm[i]), w_uq, w_ukv, cos_m, sin_m,
                                        GW=GW, KVL=KVL, tm=tm, off_cq=off_cq, off_ckv=off_ckv, off_kr=off_kr)
        y_mla = _flash(qn, qr, kn, kro, mv, B=B, S=S, H=H, tq=tq, tk=tk, ke_shared=True)
        fbias = jnp.pad(fox_forget_bias[i], (0, LANES - H)).reshape(1, LANES)
        qe, ke = _fox_prep(logits, fbias, S=S, GW=GW, tm=tm)
        hb = off_fox // HEAD_DIM
        y_fox = _flash(proj, qe, proj, ke, proj, B=B, S=S, H=H, tq=tq, tk=tk,
                       qm_off=hb, km_off=hb + H, v_off=hb + 2 * H)
        gb = conv_glu_bias[i]
        y_dw = _dwconv(proj, row(gb[:GW]), row(gb[GW:]), dw, row(conv_dw_bias[i]),
                       S=S, GW=GW, tm=tm, off_a=off_conv, off_b=off_conv + GW)
        y_conv = _conv_pw(y_dw, row(conv_ln_gain[i]), row(conv_ln_bias[i]), conv_w_pw[i].astype(BF16),
                          row(conv_pw_bias[i]), tm=tm)
        h = _out_proj([y_ret, y_mla, y_fox, y_conv], _to_bf16(w_out, i), h, row(mix_norm_post[i]), tm=tm, tn=tn)

        h = _ffn(h, row(ffn2_norm_pre[i]), *ffn_w[1], row(ffn2_norm_post[i]), tm=tm, tf=tf)

        h = _ple(h, row(ple_norm_in[i]), _to_bf16(ple_w_gate, i), p[i].reshape(T, -1).astype(BF16),
                 ple_w_proj[i].astype(BF16), row(ple_norm_post[i]), tm=tm, tn=tn)

    return h.reshape(B, S, D)
```

```python
import functools

import numpy as np
import jax
import jax.numpy as jnp
from jax import lax
from jax.experimental import pallas as pl
from jax.experimental.pallas import tpu as pltpu

F32 = jnp.float32
BF16 = jnp.bfloat16

HEAD_DIM = 128
LANES = 128
N_GROUPS = 4
MLA_ROPE = 64
CONV_K = 31
CONV_HALO = 32
ROPE_THETA = 10000.0
NORM_EPS = 1e-6
LN_EPS = 1e-5
NEG_BIG = -1e30
VMEM_LIMIT = 60 * 1024 * 1024


def _cparams(n_axes):
    return pltpu.CompilerParams(dimension_semantics=("arbitrary",) * n_axes,
                                vmem_limit_bytes=VMEM_LIMIT)


def _tile(n, pref):
    if n <= pref:
        return n
    t = pref - pref % LANES
    while t >= LANES:
        if n % t == 0:
            return t
        t -= LANES
    raise ValueError(f"no lane-aligned tile for {n}")


def _rms(x, gain, eps=NORM_EPS):
    return x * lax.rsqrt(jnp.mean(x * x, axis=-1, keepdims=True) + eps) * gain


def _silu(x):
    return x * jax.nn.sigmoid(x)


ROW_CHUNK = 64
COL_CHUNK = 512


def _for_row_chunks(n_rows, fn):
    chunk = min(ROW_CHUNK, n_rows)

    def body(r, carry):
        fn(pl.ds(pl.multiple_of(r * chunk, chunk), chunk))
        return carry

    lax.fori_loop(0, n_rows // chunk, body, 0)


def _col_slices(n_cols):
    cc = min(COL_CHUNK, n_cols)
    return [slice(c * cc, (c + 1) * cc) for c in range(n_cols // cc)]


def _row_inv_rms(src_ref, rows):
    n_cols = src_ref.shape[1]
    ss = None
    for sl in _col_slices(n_cols):
        x = src_ref[rows, sl]
        part = jnp.sum(x * x, axis=-1, keepdims=True)
        ss = part if ss is None else ss + part
    return lax.rsqrt(ss * (1.0 / n_cols) + NORM_EPS)


def _rms_to_bf16(src_ref, gain_ref, dst_ref):
    def chunk(rows):
        r = _row_inv_rms(src_ref, rows)
        for sl in _col_slices(src_ref.shape[1]):
            dst_ref[rows, sl] = (src_ref[rows, sl] * r * gain_ref[:, sl]).astype(BF16)

    _for_row_chunks(src_ref.shape[0], chunk)


def _add_scaled_rows(h_ref, gain_ref, o_ref, rows, r):
    for sl in _col_slices(o_ref.shape[1]):
        o_ref[rows, sl] = h_ref[rows, sl] + o_ref[rows, sl] * r * gain_ref[:, sl]


CAST_BLOCK_BYTES = 8 * 1024 * 1024
BF16_SUBLANES = 16


def _cast_kernel(x_ref, o_ref):
    o_ref[...] = x_ref[...].astype(BF16)


def _to_bf16(w_stack, layer):
    _, R, C = w_stack.shape
    rows = R
    for cand in range(BF16_SUBLANES, R + 1, BF16_SUBLANES):
        if R % cand == 0 and cand * C * 4 <= CAST_BLOCK_BYTES:
            rows = cand
    if rows == R and R * C * 4 > CAST_BLOCK_BYTES:
        return w_stack[layer].astype(BF16)
    return pl.pallas_call(
        _cast_kernel,
        grid=(R // rows,),
        in_specs=[pl.BlockSpec((None, rows, C), lambda r: (layer, r, 0))],
        out_specs=pl.BlockSpec((rows, C), lambda r: (r, 0)),
        out_shape=jax.ShapeDtypeStruct((R, C), BF16),
        compiler_params=_cparams(1),
        name="cast_bf16",
    )(w_stack)


W_IN_ROWS = 128


def _mix_in_weights_kernel(x_ref, s_ref, o_ref, *, sp, off_kr, n_main):
    rows = x_ref.shape[0]
    half = MLA_ROPE // 2

    def piece(a, b):
        return (x_ref[:, a:b] * s_ref[:, a:b]).astype(BF16)

    o_ref[:, :sp[5]] = piece(0, sp[5])
    kr = piece(sp[5], sp[6])
    gap = jnp.zeros((rows, LANES // 2 - half), BF16)
    o_ref[:, off_kr:off_kr + LANES] = jnp.concatenate([kr[:, :half], gap, kr[:, half:], gap], axis=1)
    off_fox = off_kr + LANES
    off_conv = off_fox + sp[9] - sp[6]
    o_ref[:, off_fox:off_conv] = piece(sp[6], sp[9])
    o_ref[:, off_conv:n_main] = piece(sp[10], sp[11])
    n_f = sp[10] - sp[9]
    o_ref[:, n_main:] = jnp.concatenate(
        [piece(sp[9], sp[10]), jnp.zeros((rows, o_ref.shape[1] - n_main - n_f), BF16)], axis=1)


def _mix_in_weights(w_in, col_scale, layer, *, sp, off_kr, n_main, n_pad):
    _, D, n_in = w_in.shape
    rows = min(W_IN_ROWS, D)
    return pl.pallas_call(
        functools.partial(_mix_in_weights_kernel, sp=sp, off_kr=off_kr, n_main=n_main),
        grid=(D // rows,),
        in_specs=[pl.BlockSpec((None, rows, n_in), lambda r: (layer, r, 0)),
                  pl.BlockSpec((1, n_in), lambda r: (0, 0))],
        out_specs=pl.BlockSpec((rows, n_pad), lambda r: (r, 0)),
        out_shape=jax.ShapeDtypeStruct((D, n_pad), BF16),
        compiler_params=_cparams(1),
        name="mix_in_weights",
    )(w_in, col_scale)


def _ffn_kernel(h_ref, gpre_ref, wg_ref, wu_ref, wd_ref, gpost_ref, *rest):
    if len(rest) == 2:
        o_ref, xn_ref = rest
    else:
        cast_in, (o_ref, *cast_out), xn_ref = rest[:3], rest[3:7], rest[7]
        for src, dst in zip(cast_in, cast_out):
            dst[...] = src[...].astype(BF16)
    j = pl.program_id(1)

    @pl.when(j == 0)
    def _():
        _rms_to_bf16(h_ref, gpre_ref, xn_ref)
        o_ref[...] = jnp.zeros_like(o_ref)

    xn = xn_ref[...]
    g = jnp.dot(xn, wg_ref[...], preferred_element_type=F32)
    u = jnp.dot(xn, wu_ref[...], preferred_element_type=F32)
    a = (_silu(g) * u).astype(BF16)
    d_model = o_ref.shape[1]
    cc = min(COL_CHUNK, d_model)
    for n in range(d_model // cc):
        sl = slice(n * cc, (n + 1) * cc)
        o_ref[:, sl] += jnp.dot(a, wd_ref[:, sl], preferred_element_type=F32)

    @pl.when(j == pl.num_programs(1) - 1)
    def _():
        def chunk(rows):
            _add_scaled_rows(h_ref, gpost_ref, o_ref, rows, 0.5 * _row_inv_rms(o_ref, rows))

        _for_row_chunks(o_ref.shape[0], chunk)


def _can_cast_in_ffn(T, D, tm):
    rows = D // (T // tm)
    return D % (T // tm) == 0 and rows % LANES == 0


def _ffn(h, g_pre, wg, wu, wd, g_post, *, tm, tf, next_weights=None):
    T, D = h.shape
    F = wg.shape[1]
    ni = T // tm
    in_specs = [
        pl.BlockSpec((tm, D), lambda i, j: (i, 0)),
        pl.BlockSpec((1, D), lambda i, j: (0, 0)),
        pl.BlockSpec((D, tf), lambda i, j: (0, j)),
        pl.BlockSpec((D, tf), lambda i, j: (0, j)),
        pl.BlockSpec((tf, D), lambda i, j: (j, 0)),
        pl.BlockSpec((1, D), lambda i, j: (0, 0)),
    ]
    out_specs = [pl.BlockSpec((tm, D), lambda i, j: (i, 0))]
    out_shape = [jax.ShapeDtypeStruct((T, D), F32)]
    operands = [h, g_pre, wg, wu, wd, g_post]
    if next_weights is not None:
        stacks, layer = next_weights
        rb = D // ni
        assert _can_cast_in_ffn(T, D, tm)
        in_specs += [pl.BlockSpec((None, rb, tf), lambda i, j: (layer, i, j)),
                     pl.BlockSpec((None, rb, tf), lambda i, j: (layer, i, j)),
                     pl.BlockSpec((None, tf, rb), lambda i, j: (layer, j, i))]
        out_specs += [pl.BlockSpec((rb, tf), lambda i, j: (i, j)),
                      pl.BlockSpec((rb, tf), lambda i, j: (i, j)),
                      pl.BlockSpec((tf, rb), lambda i, j: (j, i))]
        out_shape += [jax.ShapeDtypeStruct((D, F), BF16), jax.ShapeDtypeStruct((D, F), BF16),
                      jax.ShapeDtypeStruct((F, D), BF16)]
        operands += list(stacks)
    res = pl.pallas_call(
        _ffn_kernel,
        grid=(ni, F // tf),
        in_specs=in_specs,
        out_specs=out_specs,
        out_shape=out_shape,
        scratch_shapes=[pltpu.VMEM((tm, D), BF16)],
        compiler_params=_cparams(2),
        name="ffn",
    )(*operands)
    return res[0], tuple(res[1:])


def _proj_kernel(h_ref, g_ref, w_ref, o_ref, of_ref, xn_ref, *, f32_lanes):
    @pl.when(pl.program_id(1) == 0)
    def _():
        _rms_to_bf16(h_ref, g_ref, xn_ref)

    z = jnp.dot(xn_ref[...], w_ref[...], preferred_element_type=F32)
    o_ref[...] = z.astype(o_ref.dtype)

    @pl.when(pl.program_id(1) == pl.num_programs(1) - 1)
    def _():
        of_ref[...] = z[:, f32_lanes:f32_lanes + LANES]


def _proj(h, g, w, *, tm, tn, f32_col):
    T, D = h.shape
    N = w.shape[1]
    assert f32_col // tn == N // tn - 1 and f32_col % LANES == 0
    return pl.pallas_call(
        functools.partial(_proj_kernel, f32_lanes=f32_col % tn),
        grid=(T // tm, N // tn),
        in_specs=[
            pl.BlockSpec((tm, D), lambda i, j: (i, 0)),
            pl.BlockSpec((1, D), lambda i, j: (0, 0)),
            pl.BlockSpec((D, tn), lambda i, j: (0, j)),
        ],
        out_specs=[
            pl.BlockSpec((tm, tn), lambda i, j: (i, j)),
            pl.BlockSpec((tm, LANES), lambda i, j: (i, 0)),
        ],
        out_shape=[jax.ShapeDtypeStruct((T, N), BF16), jax.ShapeDtypeStruct((T, LANES), F32)],
        scratch_shapes=[pltpu.VMEM((tm, D), BF16)],
        compiler_params=_cparams(2),
        name="mix_in_proj",
    )(h, g, w)


def _rope_kernel(pos_ref, fr_ref, fm_ref, sr_ref, mc_ref, ms_ref, cr_ref, snr_ref, cm_ref, snm_ref):
    pos = pos_ref[...]
    ang_r = pos * fr_ref[...]
    cr_ref[...] = jnp.cos(ang_r)
    snr_ref[...] = jnp.sin(ang_r) * sr_ref[...]
    ang_m = pos * fm_ref[...]
    cm_ref[...] = jnp.cos(ang_m) * mc_ref[...]
    snm_ref[...] = jnp.sin(ang_m) * ms_ref[...]


def _rope_tables(pos_col, *, tm):
    T = pos_col.shape[0]
    lane = np.arange(LANES)
    half_r = HEAD_DIM // 2
    half_m = MLA_ROPE // 2
    freq_r = (ROPE_THETA ** (-(lane % half_r).astype(np.float32) / half_r)).astype(np.float32)
    sign_r = np.where(lane < half_r, -1.0, 1.0).astype(np.float32)
    active = (lane % 64) < half_m
    freq_m = (ROPE_THETA ** (-(lane % 64 % half_m).astype(np.float32) / half_m)).astype(np.float32)
    mask_c = active.astype(np.float32)
    mask_s = np.where(active, np.where(lane < 64, -1.0, 1.0), 0.0).astype(np.float32)
    consts = [jnp.asarray(c[None, :]) for c in (freq_r, freq_m, sign_r, mask_c, mask_s)]
    row = pl.BlockSpec((1, LANES), lambda i: (0, 0))
    tab = pl.BlockSpec((tm, LANES), lambda i: (i, 0))
    return pl.pallas_call(
        _rope_kernel,
        grid=(T // tm,),
        in_specs=[pl.BlockSpec((tm, 1), lambda i: (i, 0)), row, row, row, row, row],
        out_specs=[tab, tab, tab, tab],
        out_shape=[jax.ShapeDtypeStruct((T, LANES), F32)] * 4,
        compiler_params=_cparams(1),
        name="rope_tables",
    )(pos_col, *consts)


def _rot(x, cos, sin):
    return x * cos + pltpu.roll(x, LANES // 2, 1) * sin


def _ret_kernel(q_ref, k_ref, v_ref, g_ref, cos_ref, sin_ref, dm_ref, qd_ref, kd_ref, cd_ref,
                gain_ref, bias_ref, o_ref, st_ref, *, n_heads):
    @pl.when(pl.program_id(1) == 0)
    def _():
        st_ref[...] = jnp.zeros_like(st_ref)

    cos = cos_ref[...]
    sin = sin_ref[...]
    for h in range(n_heads):
        sl = slice(h * HEAD_DIM, (h + 1) * HEAD_DIM)
        qr = _rot(q_ref[:, sl].astype(F32), cos, sin)
        kr = _rot(k_ref[:, sl].astype(F32), cos, sin)
        vb = v_ref[:, sl]
        sc = lax.dot_general(qr.astype(BF16), kr.astype(BF16), (((1,), (1,)), ((), ())),
                             preferred_element_type=F32) * dm_ref[h]
        inner = jnp.dot(sc.astype(BF16), vb, preferred_element_type=F32)
        st = st_ref[h]
        cross = jnp.dot((qr * qd_ref[h]).astype(BF16), st.astype(BF16), preferred_element_type=F32)
        kdt = (kr * kd_ref[h]).T.astype(BF16)
        st_ref[h] = st * cd_ref[h] + jnp.dot(kdt, vb, preferred_element_type=F32)
        o = inner + cross
        mu = jnp.mean(o, axis=-1, keepdims=True)
        oc = o - mu
        var = jnp.mean(oc * oc, axis=-1, keepdims=True)
        y = (oc * lax.rsqrt(var + LN_EPS)) * gain_ref[:, sl] + bias_ref[:, sl]
        o_ref[:, sl] = (y * _silu(g_ref[:, sl].astype(F32))).astype(o_ref.dtype)


def _retention(proj, cos_r, sin_r, gain, bias, *, B, S, GW, C):
    T = B * S
    H = GW // HEAD_DIM
    nC = S // C
    idx = np.arange(C, dtype=np.float64)
    log_gamma = np.log(1.0 - 2.0 ** (-5.0 - np.arange(H, dtype=np.float64)))[:, None, None]
    diff = idx[:, None] - idx[None, :]
    dmat = np.where(diff >= 0, np.exp(log_gamma * np.maximum(diff, 0.0)), 0.0)
    qdec = np.broadcast_to(np.exp(log_gamma * (idx + 1.0)[None, :, None]), (H, C, LANES))
    kdec = np.broadcast_to(np.exp(log_gamma * (C - 1.0 - idx)[None, :, None]), (H, C, LANES))
    cdec = np.broadcast_to(np.exp(log_gamma * C), (H, 1, LANES))
    consts = [jnp.asarray(np.ascontiguousarray(c), F32) for c in (dmat, qdec, kdec, cdec)]

    def col(c):
        return pl.BlockSpec((C, GW), lambda b, i, c=c: (b * nC + i, c))

    def const(shape):
        return pl.BlockSpec(shape, lambda b, i: (0,) * len(shape))

    tab = pl.BlockSpec((C, LANES), lambda b, i: (b * nC + i, 0))
    return pl.pallas_call(
        functools.partial(_ret_kernel, n_heads=H),
        grid=(B, nC),
        in_specs=[col(0), col(1), col(2), col(3), tab, tab,
                  const((H, C, C)), const((H, C, LANES)), const((H, C, LANES)), const((H, 1, LANES)),
                  const((1, GW)), const((1, GW))],
        out_specs=pl.BlockSpec((C, GW), lambda b, i: (b * nC + i, 0)),
        out_shape=jax.ShapeDtypeStruct((T, GW), BF16),
        scratch_shapes=[pltpu.VMEM((H, HEAD_DIM, HEAD_DIM), F32)],
        compiler_params=_cparams(2),
        name="retention",
    )(proj, proj, proj, proj, cos_r, sin_r, *consts, gain, bias)


def _mla_prep_kernel(cq_ref, ckv_ref, kr_ref, gq_ref, gkv_ref, wq_ref, wkv_ref, cos_ref, sin_ref,
                     qn_ref, qr_ref, kn_ref, v_ref, kro_ref, *, gw):
    cos = cos_ref[...]
    sin = sin_ref[...]
    cqn = _rms(cq_ref[...].astype(F32), gq_ref[...]).astype(BF16)
    q = jnp.dot(cqn, wq_ref[...], preferred_element_type=F32)
    qn_ref[...] = q[:, :gw].astype(BF16)
    for h in range(gw // HEAD_DIM):
        sl = slice(h * HEAD_DIM, (h + 1) * HEAD_DIM)
        qr_ref[:, sl] = _rot(q[:, gw + h * HEAD_DIM:gw + (h + 1) * HEAD_DIM], cos, sin).astype(BF16)
    ckn = _rms(ckv_ref[...].astype(F32), gkv_ref[...]).astype(BF16)
    kv = jnp.dot(ckn, wkv_ref[...], preferred_element_type=F32)
    kn_ref[...] = kv[:, :gw].astype(BF16)
    v_ref[...] = kv[:, gw:].astype(BF16)
    kro_ref[...] = _rot(kr_ref[...].astype(F32), cos, sin).astype(BF16)


def _mla_prep(proj, gq, gkv, wq, wkv, cos_m, sin_m, *, GW, KVL, tm, off_cq, off_ckv, off_kr):
    T = proj.shape[0]
    row = lambda n: pl.BlockSpec((1, n), lambda i: (0, 0))
    blk = lambda n: pl.BlockSpec((tm, n), lambda i: (i, 0))
    return pl.pallas_call(
        functools.partial(_mla_prep_kernel, gw=GW),
        grid=(T // tm,),
        in_specs=[
            pl.BlockSpec((tm, GW), lambda i: (i, off_cq // GW)),
            pl.BlockSpec((tm, KVL), lambda i: (i, off_ckv // KVL)),
            pl.BlockSpec((tm, LANES), lambda i: (i, off_kr // LANES)),
            row(GW), row(KVL),
            pl.BlockSpec((GW, 2 * GW), lambda i: (0, 0)),
            pl.BlockSpec((KVL, 2 * GW), lambda i: (0, 0)),
            blk(LANES), blk(LANES),
        ],
        out_specs=[blk(GW), blk(GW), blk(GW), blk(GW), blk(LANES)],
        out_shape=[jax.ShapeDtypeStruct((T, GW), BF16)] * 4 + [jax.ShapeDtypeStruct((T, LANES), BF16)],
        compiler_params=_cparams(1),
        name="mla_prep",
    )(proj, proj, proj, gq, gkv, wq, wkv, cos_m, sin_m)


def _split3(x):
    hi = x.astype(BF16)
    r1 = x - hi.astype(F32)
    mid = r1.astype(BF16)
    lo = (r1 - mid.astype(F32)).astype(BF16)
    return hi, mid, lo


def _fox_prep_kernel(lg_ref, fb_ref, tri_ref, selq_ref, selk_ref, oneq_ref, onek_ref,
                     qe_ref, ke_ref, carry_ref, *, tiles_per_seq):
    @pl.when(pl.program_id(0) % tiles_per_seq == 0)
    def _():
        carry_ref[...] = jnp.zeros_like(carry_ref)

    x = lg_ref[...] + fb_ref[...]
    logf = jnp.minimum(x, 0.0) - jnp.log(1.0 + jnp.exp(-jnp.abs(x)))
    tri = tri_ref[...]
    cum = carry_ref[...]
    for part in _split3(logf):
        cum = cum + jnp.dot(tri, part, preferred_element_type=F32)
    carry_ref[...] = cum[cum.shape[0] - 1:, :]
    parts = jnp.concatenate(_split3(cum * LOG2E), axis=-1)
    qe_ref[...] = (jnp.dot(parts, selq_ref[...], preferred_element_type=F32) + oneq_ref[...]).astype(BF16)
    ke_ref[...] = (jnp.dot(parts, selk_ref[...], preferred_element_type=F32) + onek_ref[...]).astype(BF16)


def _fox_prep(logits, fbias, *, S, GW, tm):
    T = logits.shape[0]
    H = GW // HEAD_DIM
    tri = np.tril(np.ones((tm, tm), np.float32))
    selq = np.zeros((3 * LANES, GW), np.float32)
    selk = np.zeros((3 * LANES, GW), np.float32)
    oneq = np.zeros((1, GW), np.float32)
    onek = np.zeros((1, GW), np.float32)
    for h in range(H):
        for part in range(3):
            selq[part * LANES + h, h * HEAD_DIM + part] = 1.0
            selk[part * LANES + h, h * HEAD_DIM + 3 + part] = -1.0
            oneq[0, h * HEAD_DIM + 3 + part] = 1.0
            onek[0, h * HEAD_DIM + part] = 1.0
    const = lambda a: pl.BlockSpec(a.shape, lambda i: (0, 0))
    consts = [jnp.asarray(tri, BF16), jnp.asarray(selq, BF16), jnp.asarray(selk, BF16),
              jnp.asarray(oneq), jnp.asarray(onek)]
    return pl.pallas_call(
        functools.partial(_fox_prep_kernel, tiles_per_seq=S // tm),
        grid=(T // tm,),
        in_specs=[pl.BlockSpec((tm, LANES), lambda i: (i, 0)), pl.BlockSpec((1, LANES), lambda i: (0, 0))]
                 + [const(c) for c in consts],
        out_specs=[pl.BlockSpec((tm, GW), lambda i: (i, 0))] * 2,
        out_shape=[jax.ShapeDtypeStruct((T, GW), BF16)] * 2,
        scratch_shapes=[pltpu.VMEM((1, LANES), F32)],
        compiler_params=_cparams(1),
        name="fox_prep",
    )(logits, fbias, *consts)


ATTN_ROWS = 256
LOG2E = 1.4426950408889634


def _lane_tile(x, n):
    return x if n == 1 else jnp.concatenate([x] * n, axis=1)


def _flash_kernel(qm_ref, qe_ref, km_ref, ke_ref, v_ref, o_ref, m_ref, acc_ref, *, tq, tk):
    i = pl.program_id(2)
    rg = min(ATTN_ROWS, tq)
    m_ref[...] = jnp.full_like(m_ref, NEG_BIG)
    acc_ref[...] = jnp.zeros_like(acc_ref)

    def chain(g, k_rows, n_k, causal_tail):
        q_rows = slice(g * rg, (g + 1) * rg)
        q = jnp.concatenate([qm_ref[q_rows, :], qe_ref[q_rows, :]], axis=-1)
        k = jnp.concatenate([km_ref[k_rows, :], ke_ref[k_rows, :]], axis=-1)
        v = jnp.concatenate([v_ref[k_rows, :], jnp.ones((n_k, HEAD_DIM), BF16)], axis=-1)
        s = lax.dot_general(q, k, (((1,), (1,)), ((), ())), preferred_element_type=F32)
        if causal_tail:
            r = lax.broadcasted_iota(jnp.int32, (rg, rg), 0)
            c = lax.broadcasted_iota(jnp.int32, (rg, rg), 1)
            tail = jnp.where(c <= r, s[:, n_k - rg:], NEG_BIG)
            s = tail if n_k == rg else jnp.concatenate([s[:, :n_k - rg], tail], axis=1)
        m_prev = m_ref[q_rows, :]
        m_new = jnp.maximum(m_prev, jnp.max(s, axis=-1, keepdims=True))
        alpha = jnp.exp2(m_prev - m_new)
        p = jnp.exp2(s - _lane_tile(m_new, n_k // LANES))
        pv = jnp.dot(p.astype(BF16), v, preferred_element_type=F32)
        acc_ref[q_rows, :] = _lane_tile(alpha, 2) * acc_ref[q_rows, :] + pv
        m_ref[q_rows, :] = m_new

    blocks_per_trip = tq // tk

    def body(j, carry):
        for u in range(blocks_per_trip):
            k_rows = pl.ds(pl.multiple_of((j * blocks_per_trip + u) * tk, tk), tk)
            for g in range(tq // rg):
                chain(g, k_rows, tk, False)
        return carry

    lax.fori_loop(0, i, body, 0)
    base = pl.multiple_of(i * tq, tq)
    for g in reversed(range(tq // rg)):
        n_k = (g + 1) * rg
        chain(g, pl.ds(base, n_k), n_k, True)
    acc = acc_ref[...]
    o_ref[...] = (acc[:, :HEAD_DIM] / acc[:, HEAD_DIM:]).astype(o_ref.dtype)


def _flash(qm, qe, km, ke, v, *, B, S, H, tq, tk, qm_off=0, km_off=0, v_off=0, ke_shared=False):
    T = B * S
    nq = S // tq
    qspec = lambda off: pl.BlockSpec((tq, HEAD_DIM), lambda b, h, i: (b * nq + i, off + h))
    kspec = lambda off: pl.BlockSpec((S, HEAD_DIM), lambda b, h, i: (b, off + h))
    ke_spec = pl.BlockSpec((S, HEAD_DIM), (lambda b, h, i: (b, 0)) if ke_shared else (lambda b, h, i: (b, h)))
    return pl.pallas_call(
        functools.partial(_flash_kernel, tq=tq, tk=tk),
        grid=(B, H, nq),
        in_specs=[qspec(qm_off), qspec(0), kspec(km_off), ke_spec, kspec(v_off)],
        out_specs=pl.BlockSpec((tq, HEAD_DIM), lambda b, h, i: (b * nq + i, h)),
        out_shape=jax.ShapeDtypeStruct((T, H * HEAD_DIM), BF16),
        scratch_shapes=[pltpu.VMEM((tq, LANES), F32), pltpu.VMEM((tq, 2 * HEAD_DIM), F32)],
        compiler_params=_cparams(3),
        name="causal_attention",
    )(qm, qe, km, ke, v)


def _dwconv_kernel(a_ref, b_ref, ba_ref, bb_ref, w_ref, wb_ref, o_ref, u_ref, halo_ref, *, tm, tiles_per_seq):
    c = pl.program_id(1)

    @pl.when(pl.program_id(0) % tiles_per_seq == 0)
    def _():
        halo_ref[c] = jnp.zeros((CONV_HALO, LANES), F32)

    u = (a_ref[...].astype(F32) + ba_ref[...]) * jax.nn.sigmoid(b_ref[...].astype(F32) + bb_ref[...])
    u_ref[:CONV_HALO, :] = halo_ref[c]
    u_ref[CONV_HALO:, :] = u
    halo_ref[c] = u[tm - CONV_HALO:, :]
    acc = jnp.zeros((tm, LANES), F32) + wb_ref[...]
    for j in range(CONV_K):
        start = CONV_HALO - (CONV_K - 1) + j
        acc = acc + u_ref[start:start + tm, :] * w_ref[j:j + 1, :]
    o_ref[...] = acc


def _dwconv(proj, glu_bias_a, glu_bias_b, w_dw, b_dw, *, S, GW, tm, off_a, off_b):
    T = proj.shape[0]
    nc = GW // LANES
    row = pl.BlockSpec((1, LANES), lambda i, c: (0, c))
    return pl.pallas_call(
        functools.partial(_dwconv_kernel, tm=tm, tiles_per_seq=S // tm),
        grid=(T // tm, nc),
        in_specs=[
            pl.BlockSpec((tm, LANES), lambda i, c: (i, off_a // LANES + c)),
            pl.BlockSpec((tm, LANES), lambda i, c: (i, off_b // LANES + c)),
            row, row,
            pl.BlockSpec((CONV_HALO, LANES), lambda i, c: (0, c)),
            row,
        ],
        out_specs=pl.BlockSpec((tm, LANES), lambda i, c: (i, c)),
        out_shape=jax.ShapeDtypeStruct((T, GW), F32),
        scratch_shapes=[pltpu.VMEM((tm + CONV_HALO, LANES), F32), pltpu.VMEM((nc, CONV_HALO, LANES), F32)],
        compiler_params=_cparams(2),
        name="glu_dwconv",
    )(proj, proj, glu_bias_a, glu_bias_b, w_dw, b_dw)


def _conv_pw_kernel(y_ref, g_ref, b_ref, w_ref, pb_ref, o_ref):
    y = y_ref[...]
    mu = jnp.mean(y, axis=-1, keepdims=True)
    yc = y - mu
    var = jnp.mean(yc * yc, axis=-1, keepdims=True)
    z = _silu(yc * lax.rsqrt(var + LN_EPS) * g_ref[...] + b_ref[...]).astype(BF16)
    o_ref[...] = (jnp.dot(z, w_ref[...], preferred_element_type=F32) + pb_ref[...]).astype(o_ref.dtype)


def _conv_pw(y, gain, bias, w_pw, pw_bias, *, tm):
    T, GW = y.shape
    row = pl.BlockSpec((1, GW), lambda i: (0, 0))
    blk = pl.BlockSpec((tm, GW), lambda i: (i, 0))
    return pl.pallas_call(
        _conv_pw_kernel,
        grid=(T // tm,),
        in_specs=[blk, row, row, pl.BlockSpec((GW, GW), lambda i: (0, 0)), row],
        out_specs=blk,
        out_shape=jax.ShapeDtypeStruct((T, GW), BF16),
        compiler_params=_cparams(1),
        name="conv_ln_pointwise",
    )(y, gain, bias, w_pw, pw_bias)


def _park_tile(o_ref, ss_ref, z, j, tn):
    o_ref[:, pl.ds(pl.multiple_of(j * tn, tn), tn)] = z
    ss_ref[...] += jnp.sum(z * z, axis=-1, keepdims=True)


def _finish_rows(h_ref, gain_ref, o_ref, ss_ref):
    d_model = o_ref.shape[1]

    def chunk(rows):
        r = lax.rsqrt(ss_ref[rows, :] * (1.0 / d_model) + NORM_EPS)
        _add_scaled_rows(h_ref, gain_ref, o_ref, rows, r)

    _for_row_chunks(o_ref.shape[0], chunk)


def _out_proj_kernel(y0_ref, y1_ref, y2_ref, y3_ref, w_ref, h_ref, g_ref, o_ref, ss_ref,
                     *, gw, tn, n_tiles):
    j = pl.program_id(1)

    @pl.when(j == 0)
    def _():
        ss_ref[...] = jnp.zeros_like(ss_ref)

    z = jnp.dot(y0_ref[...], w_ref[0:gw, :], preferred_element_type=F32)
    z += jnp.dot(y1_ref[...], w_ref[gw:2 * gw, :], preferred_element_type=F32)
    z += jnp.dot(y2_ref[...], w_ref[2 * gw:3 * gw, :], preferred_element_type=F32)
    z += jnp.dot(y3_ref[...], w_ref[3 * gw:4 * gw, :], preferred_element_type=F32)
    _park_tile(o_ref, ss_ref, z, j, tn)

    @pl.when(j == n_tiles - 1)
    def _():
        _finish_rows(h_ref, g_ref, o_ref, ss_ref)


def _out_proj(ys, w, h, g, *, tm, tn):
    T, D = h.shape
    GW = ys[0].shape[1]
    n_tiles = D // tn
    yspec = pl.BlockSpec((tm, GW), lambda i, j: (i, 0))
    return pl.pallas_call(
        functools.partial(_out_proj_kernel, gw=GW, tn=tn, n_tiles=n_tiles),
        grid=(T // tm, n_tiles),
        in_specs=[yspec, yspec, yspec, yspec,
                  pl.BlockSpec((N_GROUPS * GW, tn), lambda i, j: (0, j)),
                  pl.BlockSpec((tm, D), lambda i, j: (i, 0)),
                  pl.BlockSpec((1, D), lambda i, j: (0, 0))],
        out_specs=pl.BlockSpec((tm, D), lambda i, j: (i, 0)),
        out_shape=jax.ShapeDtypeStruct((T, D), F32),
        scratch_shapes=[pltpu.VMEM((tm, 1), F32)],
        compiler_params=_cparams(2),
        name="mix_out_proj",
    )(*ys, w, h, g)


def _ple_kernel(h_ref, gin_ref, wg_ref, p_ref, wp_ref, gpost_ref, o_ref, xn_ref, ss_ref, *, tn, n_tiles):
    j = pl.program_id(1)

    @pl.when(j == 0)
    def _():
        _rms_to_bf16(h_ref, gin_ref, xn_ref)
        ss_ref[...] = jnp.zeros_like(ss_ref)

    gate = jax.nn.sigmoid(jnp.dot(xn_ref[...], wg_ref[...], preferred_element_type=F32))
    z = gate * jnp.dot(p_ref[...], wp_ref[...], preferred_element_type=F32)
    _park_tile(o_ref, ss_ref, z, j, tn)

    @pl.when(j == n_tiles - 1)
    def _():
        _finish_rows(h_ref, gpost_ref, o_ref, ss_ref)


def _ple(h, g_in, wg, p, wp, g_post, *, tm, tn):
    T, D = h.shape
    P = p.shape[1]
    n_tiles = D // tn
    return pl.pallas_call(
        functools.partial(_ple_kernel, tn=tn, n_tiles=n_tiles),
        grid=(T // tm, n_tiles),
        in_specs=[pl.BlockSpec((tm, D), lambda i, j: (i, 0)),
                  pl.BlockSpec((1, D), lambda i, j: (0, 0)),
                  pl.BlockSpec((D, tn), lambda i, j: (0, j)),
                  pl.BlockSpec((tm, P), lambda i, j: (i, 0)),
                  pl.BlockSpec((P, tn), lambda i, j: (0, j)),
                  pl.BlockSpec((1, D), lambda i, j: (0, 0))],
        out_specs=pl.BlockSpec((tm, D), lambda i, j: (i, 0)),
        out_shape=jax.ShapeDtypeStruct((T, D), F32),
        scratch_shapes=[pltpu.VMEM((tm, D), BF16), pltpu.VMEM((tm, 1), F32)],
        compiler_params=_cparams(2),
        name="ple",
    )(h, g_in, wg, p, wp, g_post)


def _rope_lane_layout(w, half):
    z = jnp.zeros((w.shape[0], LANES // 2 - half), w.dtype)
    return jnp.concatenate([w[:, :half], z, w[:, half:], z], axis=1)


def _plan(D, F, S, T):
    return dict(
        tm=_tile(T if T < S else S, 512),
        tf=_tile(F, 256),
        tn=_tile(D, 1024),
        tn_proj=1024,
        tq=_tile(S, 2048),
        tk=_tile(S, 512),
        c_ret=_tile(S, 256),
    )


def kernel(x, p, positions, ffn1_norm_pre, ffn1_w_gate, ffn1_w_up, ffn1_w_down, ffn1_norm_post, mix_norm_pre, w_in, mla_q_norm, mla_w_uq, mla_kv_norm, mla_w_ukv, ret_gn_gain, ret_gn_bias, fox_forget_bias, conv_glu_bias, conv_dw, conv_dw_bias, conv_ln_gain, conv_ln_bias, conv_w_pw, conv_pw_bias, w_out, mix_norm_post, ffn2_norm_pre, ffn2_w_gate, ffn2_w_up, ffn2_w_down, ffn2_norm_post, ple_norm_in, ple_w_gate, ple_w_proj, ple_norm_post):
    B, S, D = x.shape
    depth = p.shape[0]
    T = B * S
    F = ffn1_w_gate.shape[2]
    GW = D // N_GROUPS
    H = GW // HEAD_DIM
    KVL = mla_kv_norm.shape[1]
    NOPE = HEAD_DIM
    plan = _plan(D, F, S, T)
    tm, tf, tn, tq, tk, c_ret = (plan[n] for n in ("tm", "tf", "tn", "tq", "tk", "c_ret"))
    row = lambda v: v.reshape(1, -1).astype(F32)

    h = x.reshape(T, D)
    pos_col = positions.reshape(T, 1).astype(F32)
    cos_r, sin_r, cos_m, sin_m = _rope_tables(pos_col, tm=tm)

    sp = np.cumsum([GW] * 4 + [GW, KVL, MLA_ROPE] + [GW] * 3 + [H, 2 * GW])
    off_cq, off_ckv, off_kr = 4 * GW, 5 * GW, 5 * GW + KVL
    off_fox = off_kr + LANES
    off_conv = off_fox + 3 * GW
    n_main = off_conv + 2 * GW
    tn_proj = plan["tn_proj"]
    n_main_pad = -(-(n_main + LANES) // tn_proj) * tn_proj
    assert mla_q_norm.shape[1] == GW and off_ckv % KVL == 0 and KVL % LANES == 0
    mla_scale = (NOPE + MLA_ROPE) ** -0.5 * LOG2E
    col_scale = np.ones((int(sp[-1]),), np.float32)
    col_scale[sp[0]:sp[1]] = HEAD_DIM ** -0.5
    col_scale[sp[6]:sp[7]] = HEAD_DIM ** -0.5 * LOG2E

    ffn1_stacks = (ffn1_w_gate, ffn1_w_up, ffn1_w_down)
    ffn2_stacks = (ffn2_w_gate, ffn2_w_up, ffn2_w_down)
    cast_in_ffn = _can_cast_in_ffn(T, D, tm)
    ffn_w = tuple(_to_bf16(w_, 0) for w_ in ffn1_stacks)

    for i in range(depth):
        w_main = _mix_in_weights(w_in, jnp.asarray(col_scale[None, :]), i, sp=tuple(int(v) for v in sp),
                                 off_kr=off_kr, n_main=n_main, n_pad=n_main_pad)
        uq = mla_w_uq[i].reshape(GW, H, NOPE + MLA_ROPE) * mla_scale
        uq_nope = uq[:, :, :NOPE].reshape(GW, H * NOPE)
        uq_rope = jnp.concatenate(
            [_rope_lane_layout(uq[:, hh, NOPE:], MLA_ROPE // 2) for hh in range(H)], axis=1)
        w_uq = jnp.concatenate([uq_nope, uq_rope], axis=1).astype(BF16)
        ukv = mla_w_ukv[i].reshape(KVL, H, 2 * HEAD_DIM)
        w_ukv = jnp.concatenate([ukv[:, :, :NOPE].reshape(KVL, GW), ukv[:, :, NOPE:].reshape(KVL, GW)],
                                axis=1).astype(BF16)
        dw = jnp.pad(conv_dw[i], ((0, CONV_HALO - CONV_K), (0, 0)))

        h, ffn_w = _ffn(h, row(ffn1_norm_pre[i]), *ffn_w, row(ffn1_norm_post[i]), tm=tm, tf=tf,
                        next_weights=(ffn2_stacks, i) if cast_in_ffn else None)
        if not cast_in_ffn:
            ffn_w = tuple(_to_bf16(w_, i) for w_ in ffn2_stacks)

        proj, logits = _proj(h, row(mix_norm_pre[i]), w_main, tm=tm, tn=tn_proj, f32_col=n_main)
        y_ret = _retention(proj, cos_r, sin_r, row(ret_gn_gain[i]), row(ret_gn_bias[i]), B=B, S=S, GW=GW, C=c_ret)
        qn, qr, kn, mv, kro = _mla_prep(proj, row(mla_q_norm[i]), row(mla_kv_norm[i]), w_uq, w_ukv, cos_m, sin_m,
                                        GW=GW, KVL=KVL, tm=tm, off_cq=off_cq, off_ckv=off_ckv, off_kr=off_kr)
        y_mla = _flash(qn, qr, kn, kro, mv, B=B, S=S, H=H, tq=tq, tk=tk, ke_shared=True)
        fbias = jnp.pad(fox_forget_bias[i], (0, LANES - H)).reshape(1, LANES)
        qe, ke = _fox_prep(logits, fbias, S=S, GW=GW, tm=tm)
        hb = off_fox // HEAD_DIM
        y_fox = _flash(proj, qe, proj, ke, proj, B=B, S=S, H=H, tq=tq, tk=tk,
                       qm_off=hb, km_off=hb + H, v_off=hb + 2 * H)
        gb = conv_glu_bias[i]
        y_dw = _dwconv(proj, row(gb[:GW]), row(gb[GW:]), dw, row(conv_dw_bias[i]),
                       S=S, GW=GW, tm=tm, off_a=off_conv, off_b=off_conv + GW)
        y_conv = _conv_pw(y_dw, row(conv_ln_gain[i]), row(conv_ln_bias[i]), conv_w_pw[i].astype(BF16),
                          row(conv_pw_bias[i]), tm=tm)
        h = _out_proj([y_ret, y_mla, y_fox, y_conv], _to_bf16(w_out, i), h, row(mix_norm_post[i]), tm=tm, tn=tn)

        cast_next = cast_in_ffn and i + 1 < depth
        h, ffn_w = _ffn(h, row(ffn2_norm_pre[i]), *ffn_w, row(ffn2_norm_post[i]), tm=tm, tf=tf,
                        next_weights=(ffn1_stacks, i + 1) if cast_next else None)
        if not cast_next and i + 1 < depth:
            ffn_w = tuple(_to_bf16(w_, i + 1) for w_ in ffn1_stacks)

        h = _ple(h, row(ple_norm_in[i]), _to_bf16(ple_w_gate, i), p[i].reshape(T, -1).astype(BF16),
                 ple_w_proj[i].astype(BF16), row(ple_norm_post[i]), tm=tm, tn=tn)

    return h.reshape(B, S, D)
```

```python
import functools

import numpy as np
import jax
import jax.numpy as jnp
from jax import lax
from jax.experimental import pallas as pl
from jax.experimental.pallas import tpu as pltpu

F32 = jnp.float32
BF16 = jnp.bfloat16

HEAD_DIM = 128
LANES = 128
N_GROUPS = 4
MLA_ROPE = 64
CONV_K = 31
CONV_HALO = 32
ROPE_THETA = 10000.0
NORM_EPS = 1e-6
LN_EPS = 1e-5
NEG_BIG = -1e30
VMEM_LIMIT = 60 * 1024 * 1024


def _cparams(n_axes):
    return pltpu.CompilerParams(dimension_semantics=("arbitrary",) * n_axes,
                                vmem_limit_bytes=VMEM_LIMIT)


def _tile(n, pref):
    if n <= pref:
        return n
    t = pref - pref % LANES
    while t >= LANES:
        if n % t == 0:
            return t
        t -= LANES
    raise ValueError(f"no lane-aligned tile for {n}")


def _rms(x, gain, eps=NORM_EPS):
    return x * lax.rsqrt(jnp.mean(x * x, axis=-1, keepdims=True) + eps) * gain


def _silu(x):
    return x * jax.nn.sigmoid(x)


ROW_CHUNK = 64
COL_CHUNK = 512


def _for_row_chunks(n_rows, fn):
    chunk = min(ROW_CHUNK, n_rows)

    def body(r, carry):
        fn(pl.ds(pl.multiple_of(r * chunk, chunk), chunk))
        return carry

    lax.fori_loop(0, n_rows // chunk, body, 0)


def _col_slices(n_cols):
    cc = min(COL_CHUNK, n_cols)
    return [slice(c * cc, (c + 1) * cc) for c in range(n_cols // cc)]


def _row_inv_rms(src_ref, rows):
    n_cols = src_ref.shape[1]
    ss = None
    for sl in _col_slices(n_cols):
        x = src_ref[rows, sl]
        part = jnp.sum(x * x, axis=-1, keepdims=True)
        ss = part if ss is None else ss + part
    return lax.rsqrt(ss * (1.0 / n_cols) + NORM_EPS)


def _rms_to_bf16(src_ref, gain_ref, dst_ref):
    def chunk(rows):
        r = _row_inv_rms(src_ref, rows)
        for sl in _col_slices(src_ref.shape[1]):
            dst_ref[rows, sl] = (src_ref[rows, sl] * r * gain_ref[:, sl]).astype(BF16)

    _for_row_chunks(src_ref.shape[0], chunk)


def _add_scaled_rows(h_ref, gain_ref, o_ref, rows, r):
    for sl in _col_slices(o_ref.shape[1]):
        o_ref[rows, sl] = h_ref[rows, sl] + o_ref[rows, sl] * r * gain_ref[:, sl]


CAST_BLOCK_BYTES = 8 * 1024 * 1024
BF16_SUBLANES = 16


def _cast_kernel(x_ref, o_ref):
    o_ref[...] = x_ref[...].astype(BF16)


def _to_bf16(w_stack, layer):
    _, R, C = w_stack.shape
    rows = R
    for cand in range(BF16_SUBLANES, R + 1, BF16_SUBLANES):
        if R % cand == 0 and cand * C * 4 <= CAST_BLOCK_BYTES:
            rows = cand
    if rows == R and R * C * 4 > CAST_BLOCK_BYTES:
        return w_stack[layer].astype(BF16)
    return pl.pallas_call(
        _cast_kernel,
        grid=(R // rows,),
        in_specs=[pl.BlockSpec((None, rows, C), lambda r: (layer, r, 0))],
        out_specs=pl.BlockSpec((rows, C), lambda r: (r, 0)),
        out_shape=jax.ShapeDtypeStruct((R, C), BF16),
        compiler_params=_cparams(1),
        name="cast_bf16",
    )(w_stack)


W_IN_ROWS = 128


def _mix_in_weights_kernel(x_ref, s_ref, o_ref, *, sp, off_kr, n_main):
    rows = x_ref.shape[0]
    half = MLA_ROPE // 2

    def piece(a, b):
        return (x_ref[:, a:b] * s_ref[:, a:b]).astype(BF16)

    o_ref[:, :sp[5]] = piece(0, sp[5])
    kr = piece(sp[5], sp[6])
    gap = jnp.zeros((rows, LANES // 2 - half), BF16)
    o_ref[:, off_kr:off_kr + LANES] = jnp.concatenate([kr[:, :half], gap, kr[:, half:], gap], axis=1)
    off_fox = off_kr + LANES
    off_conv = off_fox + sp[9] - sp[6]
    o_ref[:, off_fox:off_conv] = piece(sp[6], sp[9])
    o_ref[:, off_conv:n_main] = piece(sp[10], sp[11])
    n_f = sp[10] - sp[9]
    o_ref[:, n_main:] = jnp.concatenate(
        [piece(sp[9], sp[10]), jnp.zeros((rows, o_ref.shape[1] - n_main - n_f), BF16)], axis=1)


def _mix_in_weights(w_in, col_scale, layer, *, sp, off_kr, n_main, n_pad):
    _, D, n_in = w_in.shape
    rows = min(W_IN_ROWS, D)
    return pl.pallas_call(
        functools.partial(_mix_in_weights_kernel, sp=sp, off_kr=off_kr, n_main=n_main),
        grid=(D // rows,),
        in_specs=[pl.BlockSpec((None, rows, n_in), lambda r: (layer, r, 0)),
                  pl.BlockSpec((1, n_in), lambda r: (0, 0))],
        out_specs=pl.BlockSpec((rows, n_pad), lambda r: (r, 0)),
        out_shape=jax.ShapeDtypeStruct((D, n_pad), BF16),
        compiler_params=_cparams(1),
        name="mix_in_weights",
    )(w_in, col_scale)


def _ffn_kernel(h_ref, gpre_ref, wg_ref, wu_ref, wd_ref, gpost_ref, o_ref, xn_ref):
    j = pl.program_id(1)

    @pl.when(j == 0)
    def _():
        _rms_to_bf16(h_ref, gpre_ref, xn_ref)
        o_ref[...] = jnp.zeros_like(o_ref)

    xn = xn_ref[...]
    g = jnp.dot(xn, wg_ref[...], preferred_element_type=F32)
    u = jnp.dot(xn, wu_ref[...], preferred_element_type=F32)
    a = (_silu(g) * u).astype(BF16)
    d_model = o_ref.shape[1]
    cc = min(COL_CHUNK, d_model)
    for n in range(d_model // cc):
        sl = slice(n * cc, (n + 1) * cc)
        o_ref[:, sl] += jnp.dot(a, wd_ref[:, sl], preferred_element_type=F32)

    @pl.when(j == pl.num_programs(1) - 1)
    def _():
        def chunk(rows):
            _add_scaled_rows(h_ref, gpost_ref, o_ref, rows, 0.5 * _row_inv_rms(o_ref, rows))

        _for_row_chunks(o_ref.shape[0], chunk)


def _ffn(h, g_pre, wg, wu, wd, g_post, *, tm, tf):
    T, D = h.shape
    F = wg.shape[1]
    return pl.pallas_call(
        _ffn_kernel,
        grid=(T // tm, F // tf),
        in_specs=[
            pl.BlockSpec((tm, D), lambda i, j: (i, 0)),
            pl.BlockSpec((1, D), lambda i, j: (0, 0)),
            pl.BlockSpec((D, tf), lambda i, j: (0, j)),
            pl.BlockSpec((D, tf), lambda i, j: (0, j)),
            pl.BlockSpec((tf, D), lambda i, j: (j, 0)),
            pl.BlockSpec((1, D), lambda i, j: (0, 0)),
        ],
        out_specs=pl.BlockSpec((tm, D), lambda i, j: (i, 0)),
        out_shape=jax.ShapeDtypeStruct((T, D), F32),
        scratch_shapes=[pltpu.VMEM((tm, D), BF16)],
        compiler_params=_cparams(2),
        name="ffn",
    )(h, g_pre, wg, wu, wd, g_post)


RIDER_BLOCK_BYTES = 4 * 1024 * 1024


def _rider_rows(n_rows, n_cols, n_steps):
    for r in range(BF16_SUBLANES, n_rows + 1, BF16_SUBLANES):
        if n_rows % r == 0 and n_rows // r <= n_steps:
            return r if r * n_cols * 4 <= RIDER_BLOCK_BYTES else None
    return None


def _rider_specs(w_stack, layer, step_of, n_steps):
    _, n_rows, n_cols = w_stack.shape
    r = _rider_rows(n_rows, n_cols, n_steps)
    nb = n_rows // r
    src = pl.BlockSpec((None, r, n_cols), lambda *ids: (layer, (step_of(*ids) * nb) // n_steps, 0))
    dst = pl.BlockSpec((r, n_cols), lambda *ids: ((step_of(*ids) * nb) // n_steps, 0))
    return src, dst, jax.ShapeDtypeStruct((n_rows, n_cols), BF16)


def _proj_kernel(h_ref, g_ref, w_ref, o_ref, of_ref, xn_ref, *, f32_lanes):
    @pl.when(pl.program_id(1) == 0)
    def _():
        _rms_to_bf16(h_ref, g_ref, xn_ref)

    z = jnp.dot(xn_ref[...], w_ref[...], preferred_element_type=F32)
    o_ref[...] = z.astype(o_ref.dtype)

    @pl.when(pl.program_id(1) == pl.num_programs(1) - 1)
    def _():
        of_ref[...] = z[:, f32_lanes:f32_lanes + LANES]


def _proj(h, g, w, *, tm, tn, f32_col):
    T, D = h.shape
    N = w.shape[1]
    assert f32_col // tn == N // tn - 1 and f32_col % LANES == 0
    return pl.pallas_call(
        functools.partial(_proj_kernel, f32_lanes=f32_col % tn),
        grid=(T // tm, N // tn),
        in_specs=[
            pl.BlockSpec((tm, D), lambda i, j: (i, 0)),
            pl.BlockSpec((1, D), lambda i, j: (0, 0)),
            pl.BlockSpec((D, tn), lambda i, j: (0, j)),
        ],
        out_specs=[
            pl.BlockSpec((tm, tn), lambda i, j: (i, j)),
            pl.BlockSpec((tm, LANES), lambda i, j: (i, 0)),
        ],
        out_shape=[jax.ShapeDtypeStruct((T, N), BF16), jax.ShapeDtypeStruct((T, LANES), F32)],
        scratch_shapes=[pltpu.VMEM((tm, D), BF16)],
        compiler_params=_cparams(2),
        name="mix_in_proj",
    )(h, g, w)


def _rope_kernel(pos_ref, fr_ref, fm_ref, sr_ref, mc_ref, ms_ref, cr_ref, snr_ref, cm_ref, snm_ref):
    pos = pos_ref[...]
    ang_r = pos * fr_ref[...]
    cr_ref[...] = jnp.cos(ang_r)
    snr_ref[...] = jnp.sin(ang_r) * sr_ref[...]
    ang_m = pos * fm_ref[...]
    cm_ref[...] = jnp.cos(ang_m) * mc_ref[...]
    snm_ref[...] = jnp.sin(ang_m) * ms_ref[...]


def _rope_tables(pos_col, *, tm):
    T = pos_col.shape[0]
    lane = np.arange(LANES)
    half_r = HEAD_DIM // 2
    half_m = MLA_ROPE // 2
    freq_r = (ROPE_THETA ** (-(lane % half_r).astype(np.float32) / half_r)).astype(np.float32)
    sign_r = np.where(lane < half_r, -1.0, 1.0).astype(np.float32)
    active = (lane % 64) < half_m
    freq_m = (ROPE_THETA ** (-(lane % 64 % half_m).astype(np.float32) / half_m)).astype(np.float32)
    mask_c = active.astype(np.float32)
    mask_s = np.where(active, np.where(lane < 64, -1.0, 1.0), 0.0).astype(np.float32)
    consts = [jnp.asarray(c[None, :]) for c in (freq_r, freq_m, sign_r, mask_c, mask_s)]
    row = pl.BlockSpec((1, LANES), lambda i: (0, 0))
    tab = pl.BlockSpec((tm, LANES), lambda i: (i, 0))
    return pl.pallas_call(
        _rope_kernel,
        grid=(T // tm,),
        in_specs=[pl.BlockSpec((tm, 1), lambda i: (i, 0)), row, row, row, row, row],
        out_specs=[tab, tab, tab, tab],
        out_shape=[jax.ShapeDtypeStruct((T, LANES), F32)] * 4,
        compiler_params=_cparams(1),
        name="rope_tables",
    )(pos_col, *consts)


def _rot(x, cos, sin):
    return x * cos + pltpu.roll(x, LANES // 2, 1) * sin


def _ret_kernel(q_ref, k_ref, v_ref, g_ref, cos_ref, sin_ref, dm_ref, qd_ref, kd_ref, cd_ref,
                gain_ref, bias_ref, o_ref, st_ref, *, n_heads):
    @pl.when(pl.program_id(1) == 0)
    def _():
        st_ref[...] = jnp.zeros_like(st_ref)

    cos = cos_ref[...]
    sin = sin_ref[...]
    for h in range(n_heads):
        sl = slice(h * HEAD_DIM, (h + 1) * HEAD_DIM)
        qr = _rot(q_ref[:, sl].astype(F32), cos, sin)
        kr = _rot(k_ref[:, sl].astype(F32), cos, sin)
        vb = v_ref[:, sl]
        sc = lax.dot_general(qr.astype(BF16), kr.astype(BF16), (((1,), (1,)), ((), ())),
                             preferred_element_type=F32) * dm_ref[h]
        inner = jnp.dot(sc.astype(BF16), vb, preferred_element_type=F32)
        st = st_ref[h]
        cross = jnp.dot((qr * qd_ref[h]).astype(BF16), st.astype(BF16), preferred_element_type=F32)
        kdt = (kr * kd_ref[h]).T.astype(BF16)
        st_ref[h] = st * cd_ref[h] + jnp.dot(kdt, vb, preferred_element_type=F32)
        o = inner + cross
        mu = jnp.mean(o, axis=-1, keepdims=True)
        oc = o - mu
        var = jnp.mean(oc * oc, axis=-1, keepdims=True)
        y = (oc * lax.rsqrt(var + LN_EPS)) * gain_ref[:, sl] + bias_ref[:, sl]
        o_ref[:, sl] = (y * _silu(g_ref[:, sl].astype(F32))).astype(o_ref.dtype)


def _retention(proj, cos_r, sin_r, gain, bias, *, B, S, GW, C):
    T = B * S
    H = GW // HEAD_DIM
    nC = S // C
    idx = np.arange(C, dtype=np.float64)
    log_gamma = np.log(1.0 - 2.0 ** (-5.0 - np.arange(H, dtype=np.float64)))[:, None, None]
    diff = idx[:, None] - idx[None, :]
    dmat = np.where(diff >= 0, np.exp(log_gamma * np.maximum(diff, 0.0)), 0.0)
    qdec = np.broadcast_to(np.exp(log_gamma * (idx + 1.0)[None, :, None]), (H, C, LANES))
    kdec = np.broadcast_to(np.exp(log_gamma * (C - 1.0 - idx)[None, :, None]), (H, C, LANES))
    cdec = np.broadcast_to(np.exp(log_gamma * C), (H, 1, LANES))
    consts = [jnp.asarray(np.ascontiguousarray(c), F32) for c in (dmat, qdec, kdec, cdec)]

    def col(c):
        return pl.BlockSpec((C, GW), lambda b, i, c=c: (b * nC + i, c))

    def const(shape):
        return pl.BlockSpec(shape, lambda b, i: (0,) * len(shape))

    tab = pl.BlockSpec((C, LANES), lambda b, i: (b * nC + i, 0))
    return pl.pallas_call(
        functools.partial(_ret_kernel, n_heads=H),
        grid=(B, nC),
        in_specs=[col(0), col(1), col(2), col(3), tab, tab,
                  const((H, C, C)), const((H, C, LANES)), const((H, C, LANES)), const((H, 1, LANES)),
                  const((1, GW)), const((1, GW))],
        out_specs=pl.BlockSpec((C, GW), lambda b, i: (b * nC + i, 0)),
        out_shape=jax.ShapeDtypeStruct((T, GW), BF16),
        scratch_shapes=[pltpu.VMEM((H, HEAD_DIM, HEAD_DIM), F32)],
        compiler_params=_cparams(2),
        name="retention",
    )(proj, proj, proj, proj, cos_r, sin_r, *consts, gain, bias)


def _mla_prep_kernel(cq_ref, ckv_ref, kr_ref, gq_ref, gkv_ref, wq_ref, wkv_ref, cos_ref, sin_ref,
                     qn_ref, qr_ref, kn_ref, v_ref, kro_ref, *, gw):
    cos = cos_ref[...]
    sin = sin_ref[...]
    cqn = _rms(cq_ref[...].astype(F32), gq_ref[...]).astype(BF16)
    q = jnp.dot(cqn, wq_ref[...], preferred_element_type=F32)
    qn_ref[...] = q[:, :gw].astype(BF16)
    for h in range(gw // HEAD_DIM):
        sl = slice(h * HEAD_DIM, (h + 1) * HEAD_DIM)
        qr_ref[:, sl] = _rot(q[:, gw + h * HEAD_DIM:gw + (h + 1) * HEAD_DIM], cos, sin).astype(BF16)
    ckn = _rms(ckv_ref[...].astype(F32), gkv_ref[...]).astype(BF16)
    kv = jnp.dot(ckn, wkv_ref[...], preferred_element_type=F32)
    kn_ref[...] = kv[:, :gw].astype(BF16)
    v_ref[...] = kv[:, gw:].astype(BF16)
    kro_ref[...] = _rot(kr_ref[...].astype(F32), cos, sin).astype(BF16)


def _mla_prep(proj, gq, gkv, wq, wkv, cos_m, sin_m, *, GW, KVL, tm, off_cq, off_ckv, off_kr):
    T = proj.shape[0]
    row = lambda n: pl.BlockSpec((1, n), lambda i: (0, 0))
    blk = lambda n: pl.BlockSpec((tm, n), lambda i: (i, 0))
    return pl.pallas_call(
        functools.partial(_mla_prep_kernel, gw=GW),
        grid=(T // tm,),
        in_specs=[
            pl.BlockSpec((tm, GW), lambda i: (i, off_cq // GW)),
            pl.BlockSpec((tm, KVL), lambda i: (i, off_ckv // KVL)),
            pl.BlockSpec((tm, LANES), lambda i: (i, off_kr // LANES)),
            row(GW), row(KVL),
            pl.BlockSpec((GW, 2 * GW), lambda i: (0, 0)),
            pl.BlockSpec((KVL, 2 * GW), lambda i: (0, 0)),
            blk(LANES), blk(LANES),
        ],
        out_specs=[blk(GW), blk(GW), blk(GW), blk(GW), blk(LANES)],
        out_shape=[jax.ShapeDtypeStruct((T, GW), BF16)] * 4 + [jax.ShapeDtypeStruct((T, LANES), BF16)],
        compiler_params=_cparams(1),
        name="mla_prep",
    )(proj, proj, proj, gq, gkv, wq, wkv, cos_m, sin_m)


def _split3(x):
    hi = x.astype(BF16)
    r1 = x - hi.astype(F32)
    mid = r1.astype(BF16)
    lo = (r1 - mid.astype(F32)).astype(BF16)
    return hi, mid, lo


def _fox_prep_kernel(lg_ref, fb_ref, tri_ref, selq_ref, selk_ref, oneq_ref, onek_ref,
                     qe_ref, ke_ref, carry_ref, *, tiles_per_seq):
    @pl.when(pl.program_id(0) % tiles_per_seq == 0)
    def _():
        carry_ref[...] = jnp.zeros_like(carry_ref)

    x = lg_ref[...] + fb_ref[...]
    logf = jnp.minimum(x, 0.0) - jnp.log(1.0 + jnp.exp(-jnp.abs(x)))
    tri = tri_ref[...]
    cum = carry_ref[...]
    for part in _split3(logf):
        cum = cum + jnp.dot(tri, part, preferred_element_type=F32)
    carry_ref[...] = cum[cum.shape[0] - 1:, :]
    parts = jnp.concatenate(_split3(cum * LOG2E), axis=-1)
    qe_ref[...] = (jnp.dot(parts, selq_ref[...], preferred_element_type=F32) + oneq_ref[...]).astype(BF16)
    ke_ref[...] = (jnp.dot(parts, selk_ref[...], preferred_element_type=F32) + onek_ref[...]).astype(BF16)


def _fox_prep(logits, fbias, *, S, GW, tm):
    T = logits.shape[0]
    H = GW // HEAD_DIM
    tri = np.tril(np.ones((tm, tm), np.float32))
    selq = np.zeros((3 * LANES, GW), np.float32)
    selk = np.zeros((3 * LANES, GW), np.float32)
    oneq = np.zeros((1, GW), np.float32)
    onek = np.zeros((1, GW), np.float32)
    for h in range(H):
        for part in range(3):
            selq[part * LANES + h, h * HEAD_DIM + part] = 1.0
            selk[part * LANES + h, h * HEAD_DIM + 3 + part] = -1.0
            oneq[0, h * HEAD_DIM + 3 + part] = 1.0
            onek[0, h * HEAD_DIM + part] = 1.0
    const = lambda a: pl.BlockSpec(a.shape, lambda i: (0, 0))
    consts = [jnp.asarray(tri, BF16), jnp.asarray(selq, BF16), jnp.asarray(selk, BF16),
              jnp.asarray(oneq), jnp.asarray(onek)]
    return pl.pallas_call(
        functools.partial(_fox_prep_kernel, tiles_per_seq=S // tm),
        grid=(T // tm,),
        in_specs=[pl.BlockSpec((tm, LANES), lambda i: (i, 0)), pl.BlockSpec((1, LANES), lambda i: (0, 0))]
                 + [const(c) for c in consts],
        out_specs=[pl.BlockSpec((tm, GW), lambda i: (i, 0))] * 2,
        out_shape=[jax.ShapeDtypeStruct((T, GW), BF16)] * 2,
        scratch_shapes=[pltpu.VMEM((1, LANES), F32)],
        compiler_params=_cparams(1),
        name="fox_prep",
    )(logits, fbias, *consts)


ATTN_ROWS = 512
LOG2E = 1.4426950408889634


def _lane_tile(x, n):
    return x if n == 1 else jnp.concatenate([x] * n, axis=1)


def _flash_kernel(qm_ref, qe_ref, km_ref, ke_ref, v_ref, *rest, tq, tk, n_riders):
    rider_src, (o_ref, *rider_dst), (m_ref, acc_ref) = rest[:n_riders], rest[n_riders:2 * n_riders + 1], rest[-2:]
    i = pl.program_id(2)
    rg = min(ATTN_ROWS, tq)
    m_ref[...] = jnp.full_like(m_ref, NEG_BIG)
    acc_ref[...] = jnp.zeros_like(acc_ref)

    def chain(g, k_rows, n_k, causal_tail):
        q_rows = slice(g * rg, (g + 1) * rg)
        q = jnp.concatenate([qm_ref[q_rows, :], qe_ref[q_rows, :]], axis=-1)
        k = jnp.concatenate([km_ref[k_rows, :], ke_ref[k_rows, :]], axis=-1)
        v = jnp.concatenate([v_ref[k_rows, :], jnp.ones((n_k, HEAD_DIM), BF16)], axis=-1)
        s = lax.dot_general(q, k, (((1,), (1,)), ((), ())), preferred_element_type=F32)
        if causal_tail:
            r = lax.broadcasted_iota(jnp.int32, (rg, rg), 0)
            c = lax.broadcasted_iota(jnp.int32, (rg, rg), 1)
            tail = jnp.where(c <= r, s[:, n_k - rg:], NEG_BIG)
            s = tail if n_k == rg else jnp.concatenate([s[:, :n_k - rg], tail], axis=1)
        m_prev = m_ref[q_rows, :]
        m_new = jnp.maximum(m_prev, jnp.max(s, axis=-1, keepdims=True))
        alpha = jnp.exp2(m_prev - m_new)
        p = jnp.exp2(s - _lane_tile(m_new, n_k // LANES))
        pv = jnp.dot(p.astype(BF16), v, preferred_element_type=F32)
        acc_ref[q_rows, :] = _lane_tile(alpha, 2) * acc_ref[q_rows, :] + pv
        m_ref[q_rows, :] = m_new

    blocks_per_trip = tq // tk

    def body(j, carry):
        for u in range(blocks_per_trip):
            k_rows = pl.ds(pl.multiple_of((j * blocks_per_trip + u) * tk, tk), tk)
            for g in range(tq // rg):
                chain(g, k_rows, tk, False)
        return carry

    lax.fori_loop(0, i, body, 0)
    for src, dst in zip(rider_src, rider_dst):
        dst[...] = src[...].astype(BF16)
    base = pl.multiple_of(i * tq, tq)
    for g in reversed(range(tq // rg)):
        n_k = (g + 1) * rg
        chain(g, pl.ds(base, n_k), n_k, True)
    acc = acc_ref[...]
    o_ref[...] = (acc[:, :HEAD_DIM] / acc[:, HEAD_DIM:]).astype(o_ref.dtype)


def _flash(qm, qe, km, ke, v, *, B, S, H, tq, tk, qm_off=0, km_off=0, v_off=0, ke_shared=False, cast=None):
    T = B * S
    nq = S // tq
    n_steps = B * H * nq
    qspec = lambda off: pl.BlockSpec((tq, HEAD_DIM), lambda b, h, i: (b * nq + i, off + h))
    kspec = lambda off: pl.BlockSpec((S, HEAD_DIM), lambda b, h, i: (b, off + h))
    ke_spec = pl.BlockSpec((S, HEAD_DIM), (lambda b, h, i: (b, 0)) if ke_shared else (lambda b, h, i: (b, h)))
    in_specs = [qspec(qm_off), qspec(0), kspec(km_off), ke_spec, kspec(v_off)]
    out_specs = [pl.BlockSpec((tq, HEAD_DIM), lambda b, h, i: (b * nq + i, h))]
    out_shape = [jax.ShapeDtypeStruct((T, H * HEAD_DIM), BF16)]
    stacks, layer = cast if cast is not None else ((), 0)
    for w_stack in stacks:
        src, dst, shape = _rider_specs(w_stack, layer, lambda b, h, i: (b * H + h) * nq + i, n_steps)
        in_specs.append(src)
        out_specs.append(dst)
        out_shape.append(shape)
    res = pl.pallas_call(
        functools.partial(_flash_kernel, tq=tq, tk=tk, n_riders=len(stacks)),
        grid=(B, H, nq),
        in_specs=in_specs,
        out_specs=out_specs,
        out_shape=out_shape,
        scratch_shapes=[pltpu.VMEM((tq, LANES), F32), pltpu.VMEM((tq, 2 * HEAD_DIM), F32)],
        compiler_params=_cparams(3),
        name="causal_attention",
    )(qm, qe, km, ke, v, *stacks)
    return res[0], tuple(res[1:])


def _dwconv_kernel(a_ref, b_ref, ba_ref, bb_ref, w_ref, wb_ref, o_ref, u_ref, halo_ref, *, tm, tiles_per_seq):
    c = pl.program_id(1)

    @pl.when(pl.program_id(0) % tiles_per_seq == 0)
    def _():
        halo_ref[c] = jnp.zeros((CONV_HALO, LANES), F32)

    u = (a_ref[...].astype(F32) + ba_ref[...]) * jax.nn.sigmoid(b_ref[...].astype(F32) + bb_ref[...])
    u_ref[:CONV_HALO, :] = halo_ref[c]
    u_ref[CONV_HALO:, :] = u
    halo_ref[c] = u[tm - CONV_HALO:, :]
    acc = jnp.zeros((tm, LANES), F32) + wb_ref[...]
    for j in range(CONV_K):
        start = CONV_HALO - (CONV_K - 1) + j
        acc = acc + u_ref[start:start + tm, :] * w_ref[j:j + 1, :]
    o_ref[...] = acc


def _dwconv(proj, glu_bias_a, glu_bias_b, w_dw, b_dw, *, S, GW, tm, off_a, off_b):
    T = proj.shape[0]
    nc = GW // LANES
    row = pl.BlockSpec((1, LANES), lambda i, c: (0, c))
    return pl.pallas_call(
        functools.partial(_dwconv_kernel, tm=tm, tiles_per_seq=S // tm),
        grid=(T // tm, nc),
        in_specs=[
            pl.BlockSpec((tm, LANES), lambda i, c: (i, off_a // LANES + c)),
            pl.BlockSpec((tm, LANES), lambda i, c: (i, off_b // LANES + c)),
            row, row,
            pl.BlockSpec((CONV_HALO, LANES), lambda i, c: (0, c)),
            row,
        ],
        out_specs=pl.BlockSpec((tm, LANES), lambda i, c: (i, c)),
        out_shape=jax.ShapeDtypeStruct((T, GW), F32),
        scratch_shapes=[pltpu.VMEM((tm + CONV_HALO, LANES), F32), pltpu.VMEM((nc, CONV_HALO, LANES), F32)],
        compiler_params=_cparams(2),
        name="glu_dwconv",
    )(proj, proj, glu_bias_a, glu_bias_b, w_dw, b_dw)


def _conv_pw_kernel(y_ref, g_ref, b_ref, w_ref, pb_ref, o_ref):
    y = y_ref[...]
    mu = jnp.mean(y, axis=-1, keepdims=True)
    yc = y - mu
    var = jnp.mean(yc * yc, axis=-1, keepdims=True)
    z = _silu(yc * lax.rsqrt(var + LN_EPS) * g_ref[...] + b_ref[...]).astype(BF16)
    o_ref[...] = (jnp.dot(z, w_ref[...], preferred_element_type=F32) + pb_ref[...]).astype(o_ref.dtype)


def _conv_pw(y, gain, bias, w_pw, pw_bias, *, tm):
    T, GW = y.shape
    row = pl.BlockSpec((1, GW), lambda i: (0, 0))
    blk = pl.BlockSpec((tm, GW), lambda i: (i, 0))
    return pl.pallas_call(
        _conv_pw_kernel,
        grid=(T // tm,),
        in_specs=[blk, row, row, pl.BlockSpec((GW, GW), lambda i: (0, 0)), row],
        out_specs=blk,
        out_shape=jax.ShapeDtypeStruct((T, GW), BF16),
        compiler_params=_cparams(1),
        name="conv_ln_pointwise",
    )(y, gain, bias, w_pw, pw_bias)


def _park_tile(o_ref, ss_ref, z, j, tn):
    o_ref[:, pl.ds(pl.multiple_of(j * tn, tn), tn)] = z
    ss_ref[...] += jnp.sum(z * z, axis=-1, keepdims=True)


def _finish_rows(h_ref, gain_ref, o_ref, ss_ref):
    d_model = o_ref.shape[1]

    def chunk(rows):
        r = lax.rsqrt(ss_ref[rows, :] * (1.0 / d_model) + NORM_EPS)
        _add_scaled_rows(h_ref, gain_ref, o_ref, rows, r)

    _for_row_chunks(o_ref.shape[0], chunk)


def _out_proj_kernel(y0_ref, y1_ref, y2_ref, y3_ref, w_ref, h_ref, g_ref, o_ref, ss_ref,
                     *, gw, tn, n_tiles):
    j = pl.program_id(1)

    @pl.when(j == 0)
    def _():
        ss_ref[...] = jnp.zeros_like(ss_ref)

    z = jnp.dot(y0_ref[...], w_ref[0:gw, :], preferred_element_type=F32)
    z += jnp.dot(y1_ref[...], w_ref[gw:2 * gw, :], preferred_element_type=F32)
    z += jnp.dot(y2_ref[...], w_ref[2 * gw:3 * gw, :], preferred_element_type=F32)
    z += jnp.dot(y3_ref[...], w_ref[3 * gw:4 * gw, :], preferred_element_type=F32)
    _park_tile(o_ref, ss_ref, z, j, tn)

    @pl.when(j == n_tiles - 1)
    def _():
        _finish_rows(h_ref, g_ref, o_ref, ss_ref)


def _out_proj(ys, w, h, g, *, tm, tn):
    T, D = h.shape
    GW = ys[0].shape[1]
    n_tiles = D // tn
    yspec = pl.BlockSpec((tm, GW), lambda i, j: (i, 0))
    return pl.pallas_call(
        functools.partial(_out_proj_kernel, gw=GW, tn=tn, n_tiles=n_tiles),
        grid=(T // tm, n_tiles),
        in_specs=[yspec, yspec, yspec, yspec,
                  pl.BlockSpec((N_GROUPS * GW, tn), lambda i, j: (0, j)),
                  pl.BlockSpec((tm, D), lambda i, j: (i, 0)),
                  pl.BlockSpec((1, D), lambda i, j: (0, 0))],
        out_specs=pl.BlockSpec((tm, D), lambda i, j: (i, 0)),
        out_shape=jax.ShapeDtypeStruct((T, D), F32),
        scratch_shapes=[pltpu.VMEM((tm, 1), F32)],
        compiler_params=_cparams(2),
        name="mix_out_proj",
    )(*ys, w, h, g)


def _ple_kernel(h_ref, gin_ref, wg_ref, p_ref, wp_ref, gpost_ref, o_ref, xn_ref, ss_ref, *, tn, n_tiles):
    j = pl.program_id(1)

    @pl.when(j == 0)
    def _():
        _rms_to_bf16(h_ref, gin_ref, xn_ref)
        ss_ref[...] = jnp.zeros_like(ss_ref)

    gate = jax.nn.sigmoid(jnp.dot(xn_ref[...], wg_ref[...], preferred_element_type=F32))
    z = gate * jnp.dot(p_ref[...], wp_ref[...], preferred_element_type=F32)
    _park_tile(o_ref, ss_ref, z, j, tn)

    @pl.when(j == n_tiles - 1)
    def _():
        _finish_rows(h_ref, gpost_ref, o_ref, ss_ref)


def _ple(h, g_in, wg, p, wp, g_post, *, tm, tn):
    T, D = h.shape
    P = p.shape[1]
    n_tiles = D // tn
    return pl.pallas_call(
        functools.partial(_ple_kernel, tn=tn, n_tiles=n_tiles),
        grid=(T // tm, n_tiles),
        in_specs=[pl.BlockSpec((tm, D), lambda i, j: (i, 0)),
                  pl.BlockSpec((1, D), lambda i, j: (0, 0)),
                  pl.BlockSpec((D, tn), lambda i, j: (0, j)),
                  pl.BlockSpec((tm, P), lambda i, j: (i, 0)),
                  pl.BlockSpec((P, tn), lambda i, j: (0, j)),
                  pl.BlockSpec((1, D), lambda i, j: (0, 0))],
        out_specs=pl.BlockSpec((tm, D), lambda i, j: (i, 0)),
        out_shape=jax.ShapeDtypeStruct((T, D), F32),
        scratch_shapes=[pltpu.VMEM((tm, D), BF16), pltpu.VMEM((tm, 1), F32)],
        compiler_params=_cparams(2),
        name="ple",
    )(h, g_in, wg, p, wp, g_post)


def _rope_lane_layout(w, half):
    z = jnp.zeros((w.shape[0], LANES // 2 - half), w.dtype)
    return jnp.concatenate([w[:, :half], z, w[:, half:], z], axis=1)


def _plan(D, F, S, T):
    return dict(
        tm=_tile(T if T < S else S, 512),
        tf=_tile(F, 256),
        tn=_tile(D, 1024),
        tn_proj=1024,
        tq=_tile(S, 2048),
        tk=_tile(S, 512),
        c_ret=_tile(S, 256),
    )


def kernel(x, p, positions, ffn1_norm_pre, ffn1_w_gate, ffn1_w_up, ffn1_w_down, ffn1_norm_post, mix_norm_pre, w_in, mla_q_norm, mla_w_uq, mla_kv_norm, mla_w_ukv, ret_gn_gain, ret_gn_bias, fox_forget_bias, conv_glu_bias, conv_dw, conv_dw_bias, conv_ln_gain, conv_ln_bias, conv_w_pw, conv_pw_bias, w_out, mix_norm_post, ffn2_norm_pre, ffn2_w_gate, ffn2_w_up, ffn2_w_down, ffn2_norm_post, ple_norm_in, ple_w_gate, ple_w_proj, ple_norm_post):
    B, S, D = x.shape
    depth = p.shape[0]
    T = B * S
    F = ffn1_w_gate.shape[2]
    GW = D // N_GROUPS
    H = GW // HEAD_DIM
    KVL = mla_kv_norm.shape[1]
    NOPE = HEAD_DIM
    plan = _plan(D, F, S, T)
    tm, tf, tn, tq, tk, c_ret = (plan[n] for n in ("tm", "tf", "tn", "tq", "tk", "c_ret"))
    row = lambda v: v.reshape(1, -1).astype(F32)

    h = x.reshape(T, D)
    pos_col = positions.reshape(T, 1).astype(F32)
    cos_r, sin_r, cos_m, sin_m = _rope_tables(pos_col, tm=tm)

    sp = np.cumsum([GW] * 4 + [GW, KVL, MLA_ROPE] + [GW] * 3 + [H, 2 * GW])
    off_cq, off_ckv, off_kr = 4 * GW, 5 * GW, 5 * GW + KVL
    off_fox = off_kr + LANES
    off_conv = off_fox + 3 * GW
    n_main = off_conv + 2 * GW
    tn_proj = plan["tn_proj"]
    n_main_pad = -(-(n_main + LANES) // tn_proj) * tn_proj
    assert mla_q_norm.shape[1] == GW and off_ckv % KVL == 0 and KVL % LANES == 0
    mla_scale = (NOPE + MLA_ROPE) ** -0.5 * LOG2E
    col_scale = np.ones((int(sp[-1]),), np.float32)
    col_scale[sp[0]:sp[1]] = HEAD_DIM ** -0.5
    col_scale[sp[6]:sp[7]] = HEAD_DIM ** -0.5 * LOG2E

    ffn1_stacks = (ffn1_w_gate, ffn1_w_up, ffn1_w_down)
    ffn2_stacks = (ffn2_w_gate, ffn2_w_up, ffn2_w_down)
    n_attn_steps = B * H * (S // tq)
    ride = all(_rider_rows(w_.shape[1], w_.shape[2], n_attn_steps) is not None for w_ in ffn1_stacks)

    def ffn_weights(stacks, layer, ridden):
        return ridden if ride else tuple(_to_bf16(w_, layer) for w_ in stacks)

    ffn1_w = tuple(_to_bf16(w_, 0) for w_ in ffn1_stacks)

    for i in range(depth):
        w_main = _mix_in_weights(w_in, jnp.asarray(col_scale[None, :]), i, sp=tuple(int(v) for v in sp),
                                 off_kr=off_kr, n_main=n_main, n_pad=n_main_pad)
        uq = mla_w_uq[i].reshape(GW, H, NOPE + MLA_ROPE) * mla_scale
        uq_nope = uq[:, :, :NOPE].reshape(GW, H * NOPE)
        uq_rope = jnp.concatenate(
            [_rope_lane_layout(uq[:, hh, NOPE:], MLA_ROPE // 2) for hh in range(H)], axis=1)
        w_uq = jnp.concatenate([uq_nope, uq_rope], axis=1).astype(BF16)
        ukv = mla_w_ukv[i].reshape(KVL, H, 2 * HEAD_DIM)
        w_ukv = jnp.concatenate([ukv[:, :, :NOPE].reshape(KVL, GW), ukv[:, :, NOPE:].reshape(KVL, GW)],
                                axis=1).astype(BF16)
        dw = jnp.pad(conv_dw[i], ((0, CONV_HALO - CONV_K), (0, 0)))

        h = _ffn(h, row(ffn1_norm_pre[i]), *ffn1_w, row(ffn1_norm_post[i]), tm=tm, tf=tf)

        proj, logits = _proj(h, row(mix_norm_pre[i]), w_main, tm=tm, tn=tn_proj, f32_col=n_main)
        y_ret = _retention(proj, cos_r, sin_r, row(ret_gn_gain[i]), row(ret_gn_bias[i]), B=B, S=S, GW=GW, C=c_ret)
        qn, qr, kn, mv, kro = _mla_prep(proj, row(mla_q_norm[i]), row(mla_kv_norm[i]), w_uq, w_ukv, cos_m, sin_m,
                                        GW=GW, KVL=KVL, tm=tm, off_cq=off_cq, off_ckv=off_ckv, off_kr=off_kr)
        y_mla, ridden = _flash(qn, qr, kn, kro, mv, B=B, S=S, H=H, tq=tq, tk=tk, ke_shared=True,
                               cast=(ffn2_stacks, i) if ride else None)
        ffn2_w = ffn_weights(ffn2_stacks, i, ridden)
        fbias = jnp.pad(fox_forget_bias[i], (0, LANES - H)).reshape(1, LANES)
        qe, ke = _fox_prep(logits, fbias, S=S, GW=GW, tm=tm)
        hb = off_fox // HEAD_DIM
        more = i + 1 < depth
        y_fox, ridden = _flash(proj, qe, proj, ke, proj, B=B, S=S, H=H, tq=tq, tk=tk,
                               qm_off=hb, km_off=hb + H, v_off=hb + 2 * H,
                               cast=(ffn1_stacks, i + 1) if ride and more else None)
        if more:
            ffn1_w = ffn_weights(ffn1_stacks, i + 1, ridden)
        gb = conv_glu_bias[i]
        y_dw = _dwconv(proj, row(gb[:GW]), row(gb[GW:]), dw, row(conv_dw_bias[i]),
                       S=S, GW=GW, tm=tm, off_a=off_conv, off_b=off_conv + GW)
        y_conv = _conv_pw(y_dw, row(conv_ln_gain[i]), row(conv_ln_bias[i]), conv_w_pw[i].astype(BF16),
                          row(conv_pw_bias[i]), tm=tm)
        h = _out_proj([y_ret, y_mla, y_fox, y_conv], _to_bf16(w_out, i), h, row(mix_norm_post[i]), tm=tm, tn=tn)

        h = _ffn(h, row(ffn2_norm_pre[i]), *ffn2_w, row(ffn2_norm_post[i]), tm=tm, tf=tf)

        h = _ple(h, row(ple_norm_in[i]), _to_bf16(ple_w_gate, i), p[i].reshape(T, -1).astype(BF16),
                 ple_w_proj[i].astype(BF16), row(ple_norm_post[i]), tm=tm, tn=tn)

    return h.reshape(B, S, D)
```

```python
import functools

import numpy as np
import jax
import jax.numpy as jnp
from jax import lax
from jax.experimental import pallas as pl
from jax.experimental.pallas import tpu as pltpu

F32 = jnp.float32
BF16 = jnp.bfloat16

HEAD_DIM = 128
LANES = 128
N_GROUPS = 4
MLA_ROPE = 64
CONV_K = 31
CONV_HALO = 32
ROPE_THETA = 10000.0
NORM_EPS = 1e-6
LN_EPS = 1e-5
NEG_BIG = -1e30
VMEM_LIMIT = 60 * 1024 * 1024


def _cparams(n_axes):
    return pltpu.CompilerParams(dimension_semantics=("arbitrary",) * n_axes,
                                vmem_limit_bytes=VMEM_LIMIT)


def _tile(n, pref):
    if n <= pref:
        return n
    t = pref - pref % LANES
    while t >= LANES:
        if n % t == 0:
            return t
        t -= LANES
    raise ValueError(f"no lane-aligned tile for {n}")


def _rms(x, gain, eps=NORM_EPS):
    return x * lax.rsqrt(jnp.mean(x * x, axis=-1, keepdims=True) + eps) * gain


def _silu(x):
    return x * jax.nn.sigmoid(x)


ROW_CHUNK = 64
COL_CHUNK = 512


def _for_row_chunks(n_rows, fn):
    chunk = min(ROW_CHUNK, n_rows)

    def body(r, carry):
        fn(pl.ds(pl.multiple_of(r * chunk, chunk), chunk))
        return carry

    lax.fori_loop(0, n_rows // chunk, body, 0)


def _col_slices(n_cols):
    cc = min(COL_CHUNK, n_cols)
    return [slice(c * cc, (c + 1) * cc) for c in range(n_cols // cc)]


def _row_inv_rms(src_ref, rows):
    n_cols = src_ref.shape[1]
    ss = None
    for sl in _col_slices(n_cols):
        x = src_ref[rows, sl]
        part = jnp.sum(x * x, axis=-1, keepdims=True)
        ss = part if ss is None else ss + part
    return lax.rsqrt(ss * (1.0 / n_cols) + NORM_EPS)


def _rms_to_bf16(src_ref, gain_ref, dst_ref):
    def chunk(rows):
        r = _row_inv_rms(src_ref, rows)
        for sl in _col_slices(src_ref.shape[1]):
            dst_ref[rows, sl] = (src_ref[rows, sl] * r * gain_ref[:, sl]).astype(BF16)

    _for_row_chunks(src_ref.shape[0], chunk)


def _add_scaled_rows(h_ref, gain_ref, o_ref, rows, r):
    for sl in _col_slices(o_ref.shape[1]):
        o_ref[rows, sl] = h_ref[rows, sl] + o_ref[rows, sl] * r * gain_ref[:, sl]


CAST_BLOCK_BYTES = 8 * 1024 * 1024
BF16_SUBLANES = 16


def _cast_kernel(x_ref, o_ref):
    o_ref[...] = x_ref[...].astype(BF16)


def _to_bf16(w_stack, layer):
    _, R, C = w_stack.shape
    rows = R
    for cand in range(BF16_SUBLANES, R + 1, BF16_SUBLANES):
        if R % cand == 0 and cand * C * 4 <= CAST_BLOCK_BYTES:
            rows = cand
    if rows == R and R * C * 4 > CAST_BLOCK_BYTES:
        return w_stack[layer].astype(BF16)
    return pl.pallas_call(
        _cast_kernel,
        grid=(R // rows,),
        in_specs=[pl.BlockSpec((None, rows, C), lambda r: (layer, r, 0))],
        out_specs=pl.BlockSpec((rows, C), lambda r: (r, 0)),
        out_shape=jax.ShapeDtypeStruct((R, C), BF16),
        compiler_params=_cparams(1),
        name="cast_bf16",
    )(w_stack)


W_IN_ROWS = 128


def _mix_in_weights_kernel(x_ref, s_ref, o_ref, *, sp, off_kr, n_main):
    rows = x_ref.shape[0]
    half = MLA_ROPE // 2

    def piece(a, b):
        return (x_ref[:, a:b] * s_ref[:, a:b]).astype(BF16)

    o_ref[:, :sp[5]] = piece(0, sp[5])
    kr = piece(sp[5], sp[6])
    gap = jnp.zeros((rows, LANES // 2 - half), BF16)
    o_ref[:, off_kr:off_kr + LANES] = jnp.concatenate([kr[:, :half], gap, kr[:, half:], gap], axis=1)
    off_fox = off_kr + LANES
    off_conv = off_fox + sp[9] - sp[6]
    o_ref[:, off_fox:off_conv] = piece(sp[6], sp[9])
    o_ref[:, off_conv:n_main] = piece(sp[10], sp[11])
    n_f = sp[10] - sp[9]
    o_ref[:, n_main:] = jnp.concatenate(
        [piece(sp[9], sp[10]), jnp.zeros((rows, o_ref.shape[1] - n_main - n_f), BF16)], axis=1)


def _mix_in_weights(w_in, col_scale, layer, *, sp, off_kr, n_main, n_pad):
    _, D, n_in = w_in.shape
    rows = min(W_IN_ROWS, D)
    return pl.pallas_call(
        functools.partial(_mix_in_weights_kernel, sp=sp, off_kr=off_kr, n_main=n_main),
        grid=(D // rows,),
        in_specs=[pl.BlockSpec((None, rows, n_in), lambda r: (layer, r, 0)),
                  pl.BlockSpec((1, n_in), lambda r: (0, 0))],
        out_specs=pl.BlockSpec((rows, n_pad), lambda r: (r, 0)),
        out_shape=jax.ShapeDtypeStruct((D, n_pad), BF16),
        compiler_params=_cparams(1),
        name="mix_in_weights",
    )(w_in, col_scale)


def _ffn_kernel(h_ref, gpre_ref, wg_ref, wu_ref, wd_ref, gpost_ref, o_ref, xn_ref):
    j = pl.program_id(1)

    @pl.when(j == 0)
    def _():
        _rms_to_bf16(h_ref, gpre_ref, xn_ref)
        o_ref[...] = jnp.zeros_like(o_ref)

    xn = xn_ref[...]
    g = jnp.dot(xn, wg_ref[...], preferred_element_type=F32)
    u = jnp.dot(xn, wu_ref[...], preferred_element_type=F32)
    a = (_silu(g) * u).astype(BF16)
    d_model = o_ref.shape[1]
    cc = min(COL_CHUNK, d_model)
    for n in range(d_model // cc):
        sl = slice(n * cc, (n + 1) * cc)
        o_ref[:, sl] += jnp.dot(a, wd_ref[:, sl], preferred_element_type=F32)

    @pl.when(j == pl.num_programs(1) - 1)
    def _():
        def chunk(rows):
            _add_scaled_rows(h_ref, gpost_ref, o_ref, rows, 0.5 * _row_inv_rms(o_ref, rows))

        _for_row_chunks(o_ref.shape[0], chunk)


def _ffn(h, g_pre, wg, wu, wd, g_post, *, tm, tf):
    T, D = h.shape
    F = wg.shape[1]
    return pl.pallas_call(
        _ffn_kernel,
        grid=(T // tm, F // tf),
        in_specs=[
            pl.BlockSpec((tm, D), lambda i, j: (i, 0)),
            pl.BlockSpec((1, D), lambda i, j: (0, 0)),
            pl.BlockSpec((D, tf), lambda i, j: (0, j)),
            pl.BlockSpec((D, tf), lambda i, j: (0, j)),
            pl.BlockSpec((tf, D), lambda i, j: (j, 0)),
            pl.BlockSpec((1, D), lambda i, j: (0, 0)),
        ],
        out_specs=pl.BlockSpec((tm, D), lambda i, j: (i, 0)),
        out_shape=jax.ShapeDtypeStruct((T, D), F32),
        scratch_shapes=[pltpu.VMEM((tm, D), BF16)],
        compiler_params=_cparams(2),
        name="ffn",
    )(h, g_pre, wg, wu, wd, g_post)


RIDER_BLOCK_BYTES = 4 * 1024 * 1024


def _rider_rows(n_rows, n_cols, n_steps):
    for r in range(BF16_SUBLANES, n_rows + 1, BF16_SUBLANES):
        if n_rows % r == 0 and n_rows // r <= n_steps:
            return r if r * n_cols * 4 <= RIDER_BLOCK_BYTES else None
    return None


def _rider_specs(w_stack, layer, step_of, n_steps):
    _, n_rows, n_cols = w_stack.shape
    r = _rider_rows(n_rows, n_cols, n_steps)
    nb = n_rows // r
    src = pl.BlockSpec((None, r, n_cols), lambda *ids: (layer, (step_of(*ids) * nb) // n_steps, 0))
    dst = pl.BlockSpec((r, n_cols), lambda *ids: ((step_of(*ids) * nb) // n_steps, 0))
    return src, dst, jax.ShapeDtypeStruct((n_rows, n_cols), BF16)


def _proj_kernel(h_ref, g_ref, w_ref, o_ref, of_ref, xn_ref, *, f32_lanes):
    @pl.when(pl.program_id(1) == 0)
    def _():
        _rms_to_bf16(h_ref, g_ref, xn_ref)

    z = jnp.dot(xn_ref[...], w_ref[...], preferred_element_type=F32)
    o_ref[...] = z.astype(o_ref.dtype)

    @pl.when(pl.program_id(1) == pl.num_programs(1) - 1)
    def _():
        of_ref[...] = z[:, f32_lanes:f32_lanes + LANES]


def _proj(h, g, w, *, tm, tn, f32_col):
    T, D = h.shape
    N = w.shape[1]
    assert f32_col // tn == N // tn - 1 and f32_col % LANES == 0
    return pl.pallas_call(
        functools.partial(_proj_kernel, f32_lanes=f32_col % tn),
        grid=(T // tm, N // tn),
        in_specs=[
            pl.BlockSpec((tm, D), lambda i, j: (i, 0)),
            pl.BlockSpec((1, D), lambda i, j: (0, 0)),
            pl.BlockSpec((D, tn), lambda i, j: (0, j)),
        ],
        out_specs=[
            pl.BlockSpec((tm, tn), lambda i, j: (i, j)),
            pl.BlockSpec((tm, LANES), lambda i, j: (i, 0)),
        ],
        out_shape=[jax.ShapeDtypeStruct((T, N), BF16), jax.ShapeDtypeStruct((T, LANES), F32)],
        scratch_shapes=[pltpu.VMEM((tm, D), BF16)],
        compiler_params=_cparams(2),
        name="mix_in_proj",
    )(h, g, w)


def _rope_kernel(pos_ref, fr_ref, fm_ref, sr_ref, mc_ref, ms_ref, cr_ref, snr_ref, cm_ref, snm_ref):
    pos = pos_ref[...]
    ang_r = pos * fr_ref[...]
    cr_ref[...] = jnp.cos(ang_r)
    snr_ref[...] = jnp.sin(ang_r) * sr_ref[...]
    ang_m = pos * fm_ref[...]
    cm_ref[...] = jnp.cos(ang_m) * mc_ref[...]
    snm_ref[...] = jnp.sin(ang_m) * ms_ref[...]


def _rope_tables(pos_col, *, tm):
    T = pos_col.shape[0]
    lane = np.arange(LANES)
    half_r = HEAD_DIM // 2
    half_m = MLA_ROPE // 2
    freq_r = (ROPE_THETA ** (-(lane % half_r).astype(np.float32) / half_r)).astype(np.float32)
    sign_r = np.where(lane < half_r, -1.0, 1.0).astype(np.float32)
    active = (lane % 64) < half_m
    freq_m = (ROPE_THETA ** (-(lane % 64 % half_m).astype(np.float32) / half_m)).astype(np.float32)
    mask_c = active.astype(np.float32)
    mask_s = np.where(active, np.where(lane < 64, -1.0, 1.0), 0.0).astype(np.float32)
    consts = [jnp.asarray(c[None, :]) for c in (freq_r, freq_m, sign_r, mask_c, mask_s)]
    row = pl.BlockSpec((1, LANES), lambda i: (0, 0))
    tab = pl.BlockSpec((tm, LANES), lambda i: (i, 0))
    return pl.pallas_call(
        _rope_kernel,
        grid=(T // tm,),
        in_specs=[pl.BlockSpec((tm, 1), lambda i: (i, 0)), row, row, row, row, row],
        out_specs=[tab, tab, tab, tab],
        out_shape=[jax.ShapeDtypeStruct((T, LANES), F32)] * 4,
        compiler_params=_cparams(1),
        name="rope_tables",
    )(pos_col, *consts)


def _rot(x, cos, sin):
    return x * cos + pltpu.roll(x, LANES // 2, 1) * sin


def _ret_kernel(q_ref, k_ref, v_ref, g_ref, cos_ref, sin_ref, dm_ref, qd_ref, kd_ref, cd_ref,
                gain_ref, bias_ref, o_ref, st_ref, *, n_heads):
    @pl.when(pl.program_id(1) == 0)
    def _():
        st_ref[...] = jnp.zeros_like(st_ref)

    cos = cos_ref[...]
    sin = sin_ref[...]
    for h in range(n_heads):
        sl = slice(h * HEAD_DIM, (h + 1) * HEAD_DIM)
        qr = _rot(q_ref[:, sl].astype(F32), cos, sin)
        kr = _rot(k_ref[:, sl].astype(F32), cos, sin)
        vb = v_ref[:, sl]
        sc = lax.dot_general(qr.astype(BF16), kr.astype(BF16), (((1,), (1,)), ((), ())),
                             preferred_element_type=F32) * dm_ref[h]
        inner = jnp.dot(sc.astype(BF16), vb, preferred_element_type=F32)
        st = st_ref[h]
        cross = jnp.dot((qr * qd_ref[h]).astype(BF16), st.astype(BF16), preferred_element_type=F32)
        kdt = (kr * kd_ref[h]).T.astype(BF16)
        st_ref[h] = st * cd_ref[h] + jnp.dot(kdt, vb, preferred_element_type=F32)
        o = inner + cross
        mu = jnp.mean(o, axis=-1, keepdims=True)
        oc = o - mu
        var = jnp.mean(oc * oc, axis=-1, keepdims=True)
        y = (oc * lax.rsqrt(var + LN_EPS)) * gain_ref[:, sl] + bias_ref[:, sl]
        o_ref[:, sl] = (y * _silu(g_ref[:, sl].astype(F32))).astype(o_ref.dtype)


def _retention(proj, cos_r, sin_r, gain, bias, *, B, S, GW, C):
    T = B * S
    H = GW // HEAD_DIM
    nC = S // C
    idx = np.arange(C, dtype=np.float64)
    log_gamma = np.log(1.0 - 2.0 ** (-5.0 - np.arange(H, dtype=np.float64)))[:, None, None]
    diff = idx[:, None] - idx[None, :]
    dmat = np.where(diff >= 0, np.exp(log_gamma * np.maximum(diff, 0.0)), 0.0)
    qdec = np.broadcast_to(np.exp(log_gamma * (idx + 1.0)[None, :, None]), (H, C, LANES))
    kdec = np.broadcast_to(np.exp(log_gamma * (C - 1.0 - idx)[None, :, None]), (H, C, LANES))
    cdec = np.broadcast_to(np.exp(log_gamma * C), (H, 1, LANES))
    consts = [jnp.asarray(np.ascontiguousarray(c), F32) for c in (dmat, qdec, kdec, cdec)]

    def col(c):
        return pl.BlockSpec((C, GW), lambda b, i, c=c: (b * nC + i, c))

    def const(shape):
        return pl.BlockSpec(shape, lambda b, i: (0,) * len(shape))

    tab = pl.BlockSpec((C, LANES), lambda b, i: (b * nC + i, 0))
    return pl.pallas_call(
        functools.partial(_ret_kernel, n_heads=H),
        grid=(B, nC),
        in_specs=[col(0), col(1), col(2), col(3), tab, tab,
                  const((H, C, C)), const((H, C, LANES)), const((H, C, LANES)), const((H, 1, LANES)),
                  const((1, GW)), const((1, GW))],
        out_specs=pl.BlockSpec((C, GW), lambda b, i: (b * nC + i, 0)),
        out_shape=jax.ShapeDtypeStruct((T, GW), BF16),
        scratch_shapes=[pltpu.VMEM((H, HEAD_DIM, HEAD_DIM), F32)],
        compiler_params=_cparams(2),
        name="retention",
    )(proj, proj, proj, proj, cos_r, sin_r, *consts, gain, bias)


def _mla_prep_kernel(cq_ref, ckv_ref, kr_ref, gq_ref, gkv_ref, wq_ref, wkv_ref, cos_ref, sin_ref,
                     qn_ref, qr_ref, kn_ref, v_ref, kro_ref, *, gw):
    cos = cos_ref[...]
    sin = sin_ref[...]
    cqn = _rms(cq_ref[...].astype(F32), gq_ref[...]).astype(BF16)
    q = jnp.dot(cqn, wq_ref[...], preferred_element_type=F32)
    qn_ref[...] = q[:, :gw].astype(BF16)
    for h in range(gw // HEAD_DIM):
        sl = slice(h * HEAD_DIM, (h + 1) * HEAD_DIM)
        qr_ref[:, sl] = _rot(q[:, gw + h * HEAD_DIM:gw + (h + 1) * HEAD_DIM], cos, sin).astype(BF16)
    ckn = _rms(ckv_ref[...].astype(F32), gkv_ref[...]).astype(BF16)
    kv = jnp.dot(ckn, wkv_ref[...], preferred_element_type=F32)
    kn_ref[...] = kv[:, :gw].astype(BF16)
    v_ref[...] = kv[:, gw:].astype(BF16)
    kro_ref[...] = _rot(kr_ref[...].astype(F32), cos, sin).astype(BF16)


def _mla_prep(proj, gq, gkv, wq, wkv, cos_m, sin_m, *, GW, KVL, tm, off_cq, off_ckv, off_kr):
    T = proj.shape[0]
    row = lambda n: pl.BlockSpec((1, n), lambda i: (0, 0))
    blk = lambda n: pl.BlockSpec((tm, n), lambda i: (i, 0))
    return pl.pallas_call(
        functools.partial(_mla_prep_kernel, gw=GW),
        grid=(T // tm,),
        in_specs=[
            pl.BlockSpec((tm, GW), lambda i: (i, off_cq // GW)),
            pl.BlockSpec((tm, KVL), lambda i: (i, off_ckv // KVL)),
            pl.BlockSpec((tm, LANES), lambda i: (i, off_kr // LANES)),
            row(GW), row(KVL),
            pl.BlockSpec((GW, 2 * GW), lambda i: (0, 0)),
            pl.BlockSpec((KVL, 2 * GW), lambda i: (0, 0)),
            blk(LANES), blk(LANES),
        ],
        out_specs=[blk(GW), blk(GW), blk(GW), blk(GW), blk(LANES)],
        out_shape=[jax.ShapeDtypeStruct((T, GW), BF16)] * 4 + [jax.ShapeDtypeStruct((T, LANES), BF16)],
        compiler_params=_cparams(1),
        name="mla_prep",
    )(proj, proj, proj, gq, gkv, wq, wkv, cos_m, sin_m)


def _split3(x):
    hi = x.astype(BF16)
    r1 = x - hi.astype(F32)
    mid = r1.astype(BF16)
    lo = (r1 - mid.astype(F32)).astype(BF16)
    return hi, mid, lo


def _fox_prep_kernel(lg_ref, fb_ref, tri_ref, selq_ref, selk_ref, oneq_ref, onek_ref,
                     qe_ref, ke_ref, carry_ref, *, tiles_per_seq):
    @pl.when(pl.program_id(0) % tiles_per_seq == 0)
    def _():
        carry_ref[...] = jnp.zeros_like(carry_ref)

    x = lg_ref[...] + fb_ref[...]
    logf = jnp.minimum(x, 0.0) - jnp.log(1.0 + jnp.exp(-jnp.abs(x)))
    tri = tri_ref[...]
    cum = carry_ref[...]
    for part in _split3(logf):
        cum = cum + jnp.dot(tri, part, preferred_element_type=F32)
    carry_ref[...] = cum[cum.shape[0] - 1:, :]
    parts = jnp.concatenate(_split3(cum * LOG2E), axis=-1)
    qe_ref[...] = (jnp.dot(parts, selq_ref[...], preferred_element_type=F32) + oneq_ref[...]).astype(BF16)
    ke_ref[...] = (jnp.dot(parts, selk_ref[...], preferred_element_type=F32) + onek_ref[...]).astype(BF16)


def _fox_prep(logits, fbias, *, S, GW, tm):
    T = logits.shape[0]
    H = GW // HEAD_DIM
    tri = np.tril(np.ones((tm, tm), np.float32))
    selq = np.zeros((3 * LANES, GW), np.float32)
    selk = np.zeros((3 * LANES, GW), np.float32)
    oneq = np.zeros((1, GW), np.float32)
    onek = np.zeros((1, GW), np.float32)
    for h in range(H):
        for part in range(3):
            selq[part * LANES + h, h * HEAD_DIM + part] = 1.0
            selk[part * LANES + h, h * HEAD_DIM + 3 + part] = -1.0
            oneq[0, h * HEAD_DIM + 3 + part] = 1.0
            onek[0, h * HEAD_DIM + part] = 1.0
    const = lambda a: pl.BlockSpec(a.shape, lambda i: (0, 0))
    consts = [jnp.asarray(tri, BF16), jnp.asarray(selq, BF16), jnp.asarray(selk, BF16),
              jnp.asarray(oneq), jnp.asarray(onek)]
    return pl.pallas_call(
        functools.partial(_fox_prep_kernel, tiles_per_seq=S // tm),
        grid=(T // tm,),
        in_specs=[pl.BlockSpec((tm, LANES), lambda i: (i, 0)), pl.BlockSpec((1, LANES), lambda i: (0, 0))]
                 + [const(c) for c in consts],
        out_specs=[pl.BlockSpec((tm, GW), lambda i: (i, 0))] * 2,
        out_shape=[jax.ShapeDtypeStruct((T, GW), BF16)] * 2,
        scratch_shapes=[pltpu.VMEM((1, LANES), F32)],
        compiler_params=_cparams(1),
        name="fox_prep",
    )(logits, fbias, *consts)


ATTN_ROWS = 512
LOG2E = 1.4426950408889634


def _lane_tile(x, n):
    return x if n == 1 else jnp.concatenate([x] * n, axis=1)


def _flash_kernel(qm_ref, qe_ref, km_ref, ke_ref, v_ref, *rest, tq, tk, n_riders):
    rider_src, (o_ref, *rider_dst), (m_ref, acc_ref) = rest[:n_riders], rest[n_riders:2 * n_riders + 1], rest[-2:]
    i = pl.program_id(2)
    rg = min(ATTN_ROWS, tq)
    m_ref[...] = jnp.full_like(m_ref, NEG_BIG)
    acc_ref[...] = jnp.zeros_like(acc_ref)

    def chain(g, k_rows, n_k, causal_tail):
        q_rows = slice(g * rg, (g + 1) * rg)
        q = jnp.concatenate([qm_ref[q_rows, :], qe_ref[q_rows, :]], axis=-1)
        k = jnp.concatenate([km_ref[k_rows, :], ke_ref[k_rows, :]], axis=-1)
        v = jnp.concatenate([v_ref[k_rows, :], jnp.ones((n_k, HEAD_DIM), BF16)], axis=-1)
        s = lax.dot_general(q, k, (((1,), (1,)), ((), ())), preferred_element_type=F32)
        if causal_tail:
            r = lax.broadcasted_iota(jnp.int32, (rg, rg), 0)
            c = lax.broadcasted_iota(jnp.int32, (rg, rg), 1)
            tail = jnp.where(c <= r, s[:, n_k - rg:], NEG_BIG)
            s = tail if n_k == rg else jnp.concatenate([s[:, :n_k - rg], tail], axis=1)
        m_prev = m_ref[q_rows, :]
        m_new = jnp.maximum(m_prev, jnp.max(s, axis=-1, keepdims=True))
        alpha = jnp.exp2(m_prev - m_new)
        p = jnp.exp2(s - _lane_tile(m_new, n_k // LANES))
        pv = jnp.dot(p.astype(BF16), v, preferred_element_type=F32)
        acc_ref[q_rows, :] = _lane_tile(alpha, 2) * acc_ref[q_rows, :] + pv
        m_ref[q_rows, :] = m_new

    blocks_per_trip = tq // tk

    def body(j, carry):
        for u in range(blocks_per_trip):
            k_rows = pl.ds(pl.multiple_of((j * blocks_per_trip + u) * tk, tk), tk)
            for g in range(tq // rg):
                chain(g, k_rows, tk, False)
        return carry

    lax.fori_loop(0, i, body, 0)
    for src, dst in zip(rider_src, rider_dst):
        dst[...] = src[...].astype(BF16)
    base = pl.multiple_of(i * tq, tq)
    for g in reversed(range(tq // rg)):
        n_k = (g + 1) * rg
        chain(g, pl.ds(base, n_k), n_k, True)
    acc = acc_ref[...]
    o_ref[...] = (acc[:, :HEAD_DIM] / acc[:, HEAD_DIM:]).astype(o_ref.dtype)


def _flash(qm, qe, km, ke, v, *, B, S, H, tq, tk, qm_off=0, km_off=0, v_off=0, ke_shared=False, cast=None):
    T = B * S
    nq = S // tq
    n_steps = B * H * nq
    qspec = lambda off: pl.BlockSpec((tq, HEAD_DIM), lambda b, h, i: (b * nq + i, off + h))
    kspec = lambda off: pl.BlockSpec((S, HEAD_DIM), lambda b, h, i: (b, off + h))
    ke_spec = pl.BlockSpec((S, HEAD_DIM), (lambda b, h, i: (b, 0)) if ke_shared else (lambda b, h, i: (b, h)))
    in_specs = [qspec(qm_off), qspec(0), kspec(km_off), ke_spec, kspec(v_off)]
    out_specs = [pl.BlockSpec((tq, HEAD_DIM), lambda b, h, i: (b * nq + i, h))]
    out_shape = [jax.ShapeDtypeStruct((T, H * HEAD_DIM), BF16)]
    stacks, layer = cast if cast is not None else ((), 0)
    for w_stack in stacks:
        src, dst, shape = _rider_specs(w_stack, layer, lambda b, h, i: (b * H + h) * nq + i, n_steps)
        in_specs.append(src)
        out_specs.append(dst)
        out_shape.append(shape)
    res = pl.pallas_call(
        functools.partial(_flash_kernel, tq=tq, tk=tk, n_riders=len(stacks)),
        grid=(B, H, nq),
        in_specs=in_specs,
        out_specs=out_specs,
        out_shape=out_shape,
        scratch_shapes=[pltpu.VMEM((tq, LANES), F32), pltpu.VMEM((tq, 2 * HEAD_DIM), F32)],
        compiler_params=_cparams(3),
        name="causal_attention",
    )(qm, qe, km, ke, v, *stacks)
    return res[0], tuple(res[1:])


def _dwconv_kernel(a_ref, b_ref, ba_ref, bb_ref, w_ref, wb_ref, o_ref, u_ref, halo_ref, *, tm, tiles_per_seq):
    c = pl.program_id(1)

    @pl.when(pl.program_id(0) % tiles_per_seq == 0)
    def _():
        halo_ref[c] = jnp.zeros((CONV_HALO, LANES), F32)

    u = (a_ref[...].astype(F32) + ba_ref[...]) * jax.nn.sigmoid(b_ref[...].astype(F32) + bb_ref[...])
    u_ref[:CONV_HALO, :] = halo_ref[c]
    u_ref[CONV_HALO:, :] = u
    halo_ref[c] = u[tm - CONV_HALO:, :]
    acc = jnp.zeros((tm, LANES), F32) + wb_ref[...]
    for j in range(CONV_K):
        start = CONV_HALO - (CONV_K - 1) + j
        acc = acc + u_ref[start:start + tm, :] * w_ref[j:j + 1, :]
    o_ref[...] = acc


def _dwconv(proj, glu_bias_a, glu_bias_b, w_dw, b_dw, *, S, GW, tm, off_a, off_b):
    T = proj.shape[0]
    nc = GW // LANES
    row = pl.BlockSpec((1, LANES), lambda i, c: (0, c))
    return pl.pallas_call(
        functools.partial(_dwconv_kernel, tm=tm, tiles_per_seq=S // tm),
        grid=(T // tm, nc),
        in_specs=[
            pl.BlockSpec((tm, LANES), lambda i, c: (i, off_a // LANES + c)),
            pl.BlockSpec((tm, LANES), lambda i, c: (i, off_b // LANES + c)),
            row, row,
            pl.BlockSpec((CONV_HALO, LANES), lambda i, c: (0, c)),
            row,
        ],
        out_specs=pl.BlockSpec((tm, LANES), lambda i, c: (i, c)),
        out_shape=jax.ShapeDtypeStruct((T, GW), F32),
        scratch_shapes=[pltpu.VMEM((tm + CONV_HALO, LANES), F32), pltpu.VMEM((nc, CONV_HALO, LANES), F32)],
        compiler_params=_cparams(2),
        name="glu_dwconv",
    )(proj, proj, glu_bias_a, glu_bias_b, w_dw, b_dw)


def _conv_pw_kernel(y_ref, g_ref, b_ref, w_ref, pb_ref, o_ref):
    y = y_ref[...]
    mu = jnp.mean(y, axis=-1, keepdims=True)
    yc = y - mu
    var = jnp.mean(yc * yc, axis=-1, keepdims=True)
    z = _silu(yc * lax.rsqrt(var + LN_EPS) * g_ref[...] + b_ref[...]).astype(BF16)
    o_ref[...] = (jnp.dot(z, w_ref[...], preferred_element_type=F32) + pb_ref[...]).astype(o_ref.dtype)


def _conv_pw(y, gain, bias, w_pw, pw_bias, *, tm):
    T, GW = y.shape
    row = pl.BlockSpec((1, GW), lambda i: (0, 0))
    blk = pl.BlockSpec((tm, GW), lambda i: (i, 0))
    return pl.pallas_call(
        _conv_pw_kernel,
        grid=(T // tm,),
        in_specs=[blk, row, row, pl.BlockSpec((GW, GW), lambda i: (0, 0)), row],
        out_specs=blk,
        out_shape=jax.ShapeDtypeStruct((T, GW), BF16),
        compiler_params=_cparams(1),
        name="conv_ln_pointwise",
    )(y, gain, bias, w_pw, pw_bias)


def _park_tile(o_ref, ss_ref, z, j, tn):
    o_ref[:, pl.ds(pl.multiple_of(j * tn, tn), tn)] = z
    ss_ref[...] += jnp.sum(z * z, axis=-1, keepdims=True)


def _finish_rows(h_ref, gain_ref, o_ref, ss_ref):
    d_model = o_ref.shape[1]

    def chunk(rows):
        r = lax.rsqrt(ss_ref[rows, :] * (1.0 / d_model) + NORM_EPS)
        _add_scaled_rows(h_ref, gain_ref, o_ref, rows, r)

    _for_row_chunks(o_ref.shape[0], chunk)


def _out_proj_kernel(y0_ref, y1_ref, y2_ref, y3_ref, w_ref, h_ref, g_ref, o_ref, ss_ref,
                     *, gw, tn, n_tiles):
    j = pl.program_id(1)

    @pl.when(j == 0)
    def _():
        ss_ref[...] = jnp.zeros_like(ss_ref)

    z = jnp.dot(y0_ref[...], w_ref[0:gw, :], preferred_element_type=F32)
    z += jnp.dot(y1_ref[...], w_ref[gw:2 * gw, :], preferred_element_type=F32)
    z += jnp.dot(y2_ref[...], w_ref[2 * gw:3 * gw, :], preferred_element_type=F32)
    z += jnp.dot(y3_ref[...], w_ref[3 * gw:4 * gw, :], preferred_element_type=F32)
    _park_tile(o_ref, ss_ref, z, j, tn)

    @pl.when(j == n_tiles - 1)
    def _():
        _finish_rows(h_ref, g_ref, o_ref, ss_ref)


def _out_proj(ys, w, h, g, *, tm, tn):
    T, D = h.shape
    GW = ys[0].shape[1]
    n_tiles = D // tn
    yspec = pl.BlockSpec((tm, GW), lambda i, j: (i, 0))
    return pl.pallas_call(
        functools.partial(_out_proj_kernel, gw=GW, tn=tn, n_tiles=n_tiles),
        grid=(T // tm, n_tiles),
        in_specs=[yspec, yspec, yspec, yspec,
                  pl.BlockSpec((N_GROUPS * GW, tn), lambda i, j: (0, j)),
                  pl.BlockSpec((tm, D), lambda i, j: (i, 0)),
                  pl.BlockSpec((1, D), lambda i, j: (0, 0))],
        out_specs=pl.BlockSpec((tm, D), lambda i, j: (i, 0)),
        out_shape=jax.ShapeDtypeStruct((T, D), F32),
        scratch_shapes=[pltpu.VMEM((tm, 1), F32)],
        compiler_params=_cparams(2),
        name="mix_out_proj",
    )(*ys, w, h, g)


def _ple_kernel(h_ref, gin_ref, wg_ref, p_ref, wp_ref, gpost_ref, o_ref, xn_ref, ss_ref, *, tn, n_tiles):
    j = pl.program_id(1)

    @pl.when(j == 0)
    def _():
        _rms_to_bf16(h_ref, gin_ref, xn_ref)
        ss_ref[...] = jnp.zeros_like(ss_ref)

    gate = jax.nn.sigmoid(jnp.dot(xn_ref[...], wg_ref[...], preferred_element_type=F32))
    z = gate * jnp.dot(p_ref[...], wp_ref[...], preferred_element_type=F32)
    _park_tile(o_ref, ss_ref, z, j, tn)

    @pl.when(j == n_tiles - 1)
    def _():
        _finish_rows(h_ref, gpost_ref, o_ref, ss_ref)


def _ple(h, g_in, wg, p, wp, g_post, *, tm, tn):
    T, D = h.shape
    P = p.shape[1]
    n_tiles = D // tn
    return pl.pallas_call(
        functools.partial(_ple_kernel, tn=tn, n_tiles=n_tiles),
        grid=(T // tm, n_tiles),
        in_specs=[pl.BlockSpec((tm, D), lambda i, j: (i, 0)),
                  pl.BlockSpec((1, D), lambda i, j: (0, 0)),
                  pl.BlockSpec((D, tn), lambda i, j: (0, j)),
                  pl.BlockSpec((tm, P), lambda i, j: (i, 0)),
                  pl.BlockSpec((P, tn), lambda i, j: (0, j)),
                  pl.BlockSpec((1, D), lambda i, j: (0, 0))],
        out_specs=pl.BlockSpec((tm, D), lambda i, j: (i, 0)),
        out_shape=jax.ShapeDtypeStruct((T, D), F32),
        scratch_shapes=[pltpu.VMEM((tm, D), BF16), pltpu.VMEM((tm, 1), F32)],
        compiler_params=_cparams(2),
        name="ple",
    )(h, g_in, wg, p, wp, g_post)


def _rope_lane_layout(w, half):
    z = jnp.zeros((w.shape[0], LANES // 2 - half), w.dtype)
    return jnp.concatenate([w[:, :half], z, w[:, half:], z], axis=1)


def _plan(D, F, S, T):
    return dict(
        tm=_tile(T if T < S else S, 512),
        tf=_tile(F, 256),
        tn=_tile(D, 1024),
        tn_proj=1024,
        tq=_tile(S, 2048),
        tk=_tile(S, 512),
        c_ret=_tile(S, 256),
    )


def kernel(x, p, positions, ffn1_norm_pre, ffn1_w_gate, ffn1_w_up, ffn1_w_down, ffn1_norm_post, mix_norm_pre, w_in, mla_q_norm, mla_w_uq, mla_kv_norm, mla_w_ukv, ret_gn_gain, ret_gn_bias, fox_forget_bias, conv_glu_bias, conv_dw, conv_dw_bias, conv_ln_gain, conv_ln_bias, conv_w_pw, conv_pw_bias, w_out, mix_norm_post, ffn2_norm_pre, ffn2_w_gate, ffn2_w_up, ffn2_w_down, ffn2_norm_post, ple_norm_in, ple_w_gate, ple_w_proj, ple_norm_post):
    B, S, D = x.shape
    depth = p.shape[0]
    T = B * S
    F = ffn1_w_gate.shape[2]
    GW = D // N_GROUPS
    H = GW // HEAD_DIM
    KVL = mla_kv_norm.shape[1]
    NOPE = HEAD_DIM
    plan = _plan(D, F, S, T)
    tm, tf, tn, tq, tk, c_ret = (plan[n] for n in ("tm", "tf", "tn", "tq", "tk", "c_ret"))
    row = lambda v: v.reshape(1, -1).astype(F32)

    h = x.reshape(T, D)
    pos_col = positions.reshape(T, 1).astype(F32)
    cos_r, sin_r, cos_m, sin_m = _rope_tables(pos_col, tm=tm)

    sp = np.cumsum([GW] * 4 + [GW, KVL, MLA_ROPE] + [GW] * 3 + [H, 2 * GW])
    off_cq, off_ckv, off_kr = 4 * GW, 5 * GW, 5 * GW + KVL
    off_fox = off_kr + LANES
    off_conv = off_fox + 3 * GW
    n_main = off_conv + 2 * GW
    tn_proj = plan["tn_proj"]
    n_main_pad = -(-(n_main + LANES) // tn_proj) * tn_proj
    assert mla_q_norm.shape[1] == GW and off_ckv % KVL == 0 and KVL % LANES == 0
    mla_scale = (NOPE + MLA_ROPE) ** -0.5 * LOG2E
    col_scale = np.ones((int(sp[-1]),), np.float32)
    col_scale[sp[0]:sp[1]] = HEAD_DIM ** -0.5
    col_scale[sp[6]:sp[7]] = HEAD_DIM ** -0.5 * LOG2E

    ffn1_stacks = (ffn1_w_gate, ffn1_w_up, ffn1_w_down)
    ffn2_stacks = (ffn2_w_gate, ffn2_w_up, ffn2_w_down)
    n_attn_steps = B * H * (S // tq)
    late_stacks = ffn2_stacks + (w_out, ple_w_gate)
    ride = all(_rider_rows(w_.shape[1], w_.shape[2], n_attn_steps) is not None for w_ in late_stacks)

    def ffn_weights(stacks, layer, ridden):
        return ridden if ride else tuple(_to_bf16(w_, layer) for w_ in stacks)

    ffn1_w = tuple(_to_bf16(w_, 0) for w_ in ffn1_stacks)

    for i in range(depth):
        w_main = _mix_in_weights(w_in, jnp.asarray(col_scale[None, :]), i, sp=tuple(int(v) for v in sp),
                                 off_kr=off_kr, n_main=n_main, n_pad=n_main_pad)
        uq = mla_w_uq[i].reshape(GW, H, NOPE + MLA_ROPE) * mla_scale
        uq_nope = uq[:, :, :NOPE].reshape(GW, H * NOPE)
        uq_rope = jnp.concatenate(
            [_rope_lane_layout(uq[:, hh, NOPE:], MLA_ROPE // 2) for hh in range(H)], axis=1)
        w_uq = jnp.concatenate([uq_nope, uq_rope], axis=1).astype(BF16)
        ukv = mla_w_ukv[i].reshape(KVL, H, 2 * HEAD_DIM)
        w_ukv = jnp.concatenate([ukv[:, :, :NOPE].reshape(KVL, GW), ukv[:, :, NOPE:].reshape(KVL, GW)],
                                axis=1).astype(BF16)
        dw = jnp.pad(conv_dw[i], ((0, CONV_HALO - CONV_K), (0, 0)))

        h = _ffn(h, row(ffn1_norm_pre[i]), *ffn1_w, row(ffn1_norm_post[i]), tm=tm, tf=tf)

        proj, logits = _proj(h, row(mix_norm_pre[i]), w_main, tm=tm, tn=tn_proj, f32_col=n_main)
        y_ret = _retention(proj, cos_r, sin_r, row(ret_gn_gain[i]), row(ret_gn_bias[i]), B=B, S=S, GW=GW, C=c_ret)
        qn, qr, kn, mv, kro = _mla_prep(proj, row(mla_q_norm[i]), row(mla_kv_norm[i]), w_uq, w_ukv, cos_m, sin_m,
                                        GW=GW, KVL=KVL, tm=tm, off_cq=off_cq, off_ckv=off_ckv, off_kr=off_kr)
        y_mla, ridden = _flash(qn, qr, kn, kro, mv, B=B, S=S, H=H, tq=tq, tk=tk, ke_shared=True,
                               cast=(late_stacks, i) if ride else None)
        *ffn2_w, w_out_b, w_gate_b = ffn_weights(late_stacks, i, ridden)
        fbias = jnp.pad(fox_forget_bias[i], (0, LANES - H)).reshape(1, LANES)
        qe, ke = _fox_prep(logits, fbias, S=S, GW=GW, tm=tm)
        hb = off_fox // HEAD_DIM
        more = i + 1 < depth
        y_fox, ridden = _flash(proj, qe, proj, ke, proj, B=B, S=S, H=H, tq=tq, tk=tk,
                               qm_off=hb, km_off=hb + H, v_off=hb + 2 * H,
                               cast=(ffn1_stacks, i + 1) if ride and more else None)
        if more:
            ffn1_w = ffn_weights(ffn1_stacks, i + 1, ridden)
        gb = conv_glu_bias[i]
        y_dw = _dwconv(proj, row(gb[:GW]), row(gb[GW:]), dw, row(conv_dw_bias[i]),
                       S=S, GW=GW, tm=tm, off_a=off_conv, off_b=off_conv + GW)
        y_conv = _conv_pw(y_dw, row(conv_ln_gain[i]), row(conv_ln_bias[i]), conv_w_pw[i].astype(BF16),
                          row(conv_pw_bias[i]), tm=tm)
        h = _out_proj([y_ret, y_mla, y_fox, y_conv], w_out_b, h, row(mix_norm_post[i]), tm=tm, tn=tn)

        h = _ffn(h, row(ffn2_norm_pre[i]), *ffn2_w, row(ffn2_norm_post[i]), tm=tm, tf=tf)

        h = _ple(h, row(ple_norm_in[i]), w_gate_b, p[i].reshape(T, -1).astype(BF16),
                 ple_w_proj[i].astype(BF16), row(ple_norm_post[i]), tm=tm, tn=tn)

    return h.reshape(B, S, D)
```

```python
import functools

import numpy as np
import jax
import jax.numpy as jnp
from jax import lax
from jax.experimental import pallas as pl
from jax.experimental.pallas import tpu as pltpu

F32 = jnp.float32
BF16 = jnp.bfloat16

HEAD_DIM = 128
LANES = 128
N_GROUPS = 4
MLA_ROPE = 64
CONV_K = 31
CONV_HALO = 32
ROPE_THETA = 10000.0
NORM_EPS = 1e-6
LN_EPS = 1e-5
NEG_BIG = -1e30
VMEM_LIMIT = 60 * 1024 * 1024


def _cparams(n_axes):
    return pltpu.CompilerParams(dimension_semantics=("arbitrary",) * n_axes,
                                vmem_limit_bytes=VMEM_LIMIT)


def _tile(n, pref):
    if n <= pref:
        return n
    t = pref - pref % LANES
    while t >= LANES:
        if n % t == 0:
            return t
        t -= LANES
    raise ValueError(f"no lane-aligned tile for {n}")


def _rms(x, gain, eps=NORM_EPS):
    return x * lax.rsqrt(jnp.mean(x * x, axis=-1, keepdims=True) + eps) * gain


def _silu(x):
    return x * jax.nn.sigmoid(x)


ROW_CHUNK = 64
COL_CHUNK = 512


def _for_row_chunks(n_rows, fn):
    chunk = min(ROW_CHUNK, n_rows)

    def body(r, carry):
        fn(pl.ds(pl.multiple_of(r * chunk, chunk), chunk))
        return carry

    lax.fori_loop(0, n_rows // chunk, body, 0)


def _col_slices(n_cols):
    cc = min(COL_CHUNK, n_cols)
    return [slice(c * cc, (c + 1) * cc) for c in range(n_cols // cc)]


def _row_inv_rms(src_ref, rows):
    n_cols = src_ref.shape[1]
    ss = None
    for sl in _col_slices(n_cols):
        x = src_ref[rows, sl]
        part = jnp.sum(x * x, axis=-1, keepdims=True)
        ss = part if ss is None else ss + part
    return lax.rsqrt(ss * (1.0 / n_cols) + NORM_EPS)


def _rms_to_bf16(src_ref, gain_ref, dst_ref):
    def chunk(rows):
        r = _row_inv_rms(src_ref, rows)
        for sl in _col_slices(src_ref.shape[1]):
            dst_ref[rows, sl] = (src_ref[rows, sl] * r * gain_ref[:, sl]).astype(BF16)

    _for_row_chunks(src_ref.shape[0], chunk)


def _add_scaled_rows(h_ref, gain_ref, o_ref, rows, r):
    for sl in _col_slices(o_ref.shape[1]):
        o_ref[rows, sl] = h_ref[rows, sl] + o_ref[rows, sl] * r * gain_ref[:, sl]


CAST_BLOCK_BYTES = 8 * 1024 * 1024
BF16_SUBLANES = 16


def _cast_kernel(x_ref, o_ref):
    o_ref[...] = x_ref[...].astype(BF16)


def _to_bf16(w_stack, layer):
    _, R, C = w_stack.shape
    rows = R
    for cand in range(BF16_SUBLANES, R + 1, BF16_SUBLANES):
        if R % cand == 0 and cand * C * 4 <= CAST_BLOCK_BYTES:
            rows = cand
    if rows == R and R * C * 4 > CAST_BLOCK_BYTES:
        return w_stack[layer].astype(BF16)
    return pl.pallas_call(
        _cast_kernel,
        grid=(R // rows,),
        in_specs=[pl.BlockSpec((None, rows, C), lambda r: (layer, r, 0))],
        out_specs=pl.BlockSpec((rows, C), lambda r: (r, 0)),
        out_shape=jax.ShapeDtypeStruct((R, C), BF16),
        compiler_params=_cparams(1),
        name="cast_bf16",
    )(w_stack)


W_IN_ROWS = 128


def _mix_in_weights_kernel(x_ref, s_ref, o_ref, *, sp, off_kr, n_main):
    rows = x_ref.shape[0]
    half = MLA_ROPE // 2

    def piece(a, b):
        return (x_ref[:, a:b] * s_ref[:, a:b]).astype(BF16)

    o_ref[:, :sp[5]] = piece(0, sp[5])
    kr = piece(sp[5], sp[6])
    gap = jnp.zeros((rows, LANES // 2 - half), BF16)
    o_ref[:, off_kr:off_kr + LANES] = jnp.concatenate([kr[:, :half], gap, kr[:, half:], gap], axis=1)
    off_fox = off_kr + LANES
    off_conv = off_fox + sp[9] - sp[6]
    o_ref[:, off_fox:off_conv] = piece(sp[6], sp[9])
    o_ref[:, off_conv:n_main] = piece(sp[10], sp[11])
    n_f = sp[10] - sp[9]
    o_ref[:, n_main:] = jnp.concatenate(
        [piece(sp[9], sp[10]), jnp.zeros((rows, o_ref.shape[1] - n_main - n_f), BF16)], axis=1)


def _mix_in_weights(w_in, col_scale, layer, *, sp, off_kr, n_main, n_pad):
    _, D, n_in = w_in.shape
    rows = min(W_IN_ROWS, D)
    return pl.pallas_call(
        functools.partial(_mix_in_weights_kernel, sp=sp, off_kr=off_kr, n_main=n_main),
        grid=(D // rows,),
        in_specs=[pl.BlockSpec((None, rows, n_in), lambda r: (layer, r, 0)),
                  pl.BlockSpec((1, n_in), lambda r: (0, 0))],
        out_specs=pl.BlockSpec((rows, n_pad), lambda r: (r, 0)),
        out_shape=jax.ShapeDtypeStruct((D, n_pad), BF16),
        compiler_params=_cparams(1),
        name="mix_in_weights",
    )(w_in, col_scale)


def _mix_in_weights_t_kernel(a_ref, b_ref, s_ref, o_ref, *, segments):
    blk = pl.program_id(0)
    d = a_ref.shape[1]
    for lo, hi, pieces in segments:
        @pl.when((blk >= lo) & (blk < hi))
        def _(pieces=pieces):
            rows = [jnp.zeros((n, d), F32) if kind == "zero" else (a_ref if kind == "a" else b_ref)[r0:r0 + n, :]
                    for kind, r0, n in pieces]
            x = rows[0] if len(rows) == 1 else jnp.concatenate(rows, axis=0)
            o_ref[...] = (x.T * s_ref[...]).astype(BF16)


def _mix_in_weights_t(w_in_t, col_scale, layer, *, sp, off_kr, n_main, n_pad):
    _, n_in, D = w_in_t.shape
    half, n_f = MLA_ROPE // 2, sp[10] - sp[9]
    off_fox = off_kr + LANES
    off_conv = off_fox + sp[9] - sp[6]
    shift_fox, shift_conv = off_fox - sp[6], off_conv - sp[10]
    b_kr, b_fox, b_conv, b_tail, n_blk = (v // LANES for v in (off_kr, off_fox, off_conv, n_main, n_pad))
    src_last = (n_in - 1) // LANES
    f_blk, f_row = sp[9] // LANES, sp[9] % LANES
    shifted = lambda s: (("a", LANES - s, s), ("b", 0, LANES - s))
    segments = (
        (0, b_kr, (("a", 0, LANES),)),
        (b_kr, b_fox, (("a", 0, half), ("zero", 0, LANES // 2 - half), ("a", half, half), ("zero", 0, LANES // 2 - half))),
        (b_fox, b_conv, shifted(shift_fox)),
        (b_conv, b_tail, shifted(shift_conv)),
        (b_tail, b_tail + 1, (("a", f_row, n_f), ("zero", 0, LANES - n_f))),
        (b_tail + 1, n_blk, (("zero", 0, LANES),)),
    )
    s_dst = np.zeros((1, n_pad), np.float32)
    s_dst[0, :sp[5]] = col_scale[:sp[5]]
    s_dst[0, off_kr:off_kr + half] = col_scale[sp[5]:sp[5] + half]
    s_dst[0, off_kr + LANES // 2:off_kr + LANES // 2 + half] = col_scale[sp[5] + half:sp[6]]
    s_dst[0, off_fox:off_conv] = col_scale[sp[6]:sp[9]]
    s_dst[0, off_conv:n_main] = col_scale[sp[10]:sp[11]]
    s_dst[0, n_main:n_main + n_f] = col_scale[sp[9]:sp[10]]
    src_a = lambda b: (layer, jnp.where(b <= b_kr, b, jnp.where(b < b_tail, b - 1, f_blk)), 0)
    src_b = lambda b: (layer, jnp.clip(b, b_fox, src_last), 0)
    return pl.pallas_call(
        functools.partial(_mix_in_weights_t_kernel, segments=segments),
        grid=(n_blk,),
        in_specs=[pl.BlockSpec((None, LANES, D), src_a), pl.BlockSpec((None, LANES, D), src_b),
                  pl.BlockSpec((1, LANES), lambda b: (0, b))],
        out_specs=pl.BlockSpec((D, LANES), lambda b: (0, b)),
        out_shape=jax.ShapeDtypeStruct((D, n_pad), BF16),
        compiler_params=_cparams(1),
        name="mix_in_weights_t",
    )(w_in_t, w_in_t, jnp.asarray(s_dst))


def _can_read_w_in_transposed(sp, off_kr, n_main, n_in):
    off_fox = off_kr + LANES
    off_conv = off_fox + sp[9] - sp[6]
    tile_ok = all(v % LANES == 0 for v in (sp[5], off_kr, n_main))
    shift_ok = all(v % 8 == 0 for v in (off_fox - sp[6], off_conv - sp[10], sp[9] % LANES, sp[10] - sp[9]))
    return tile_ok and shift_ok and (sp[10] - sp[9]) + sp[9] % LANES <= LANES and n_main - LANES < n_in


def _ffn_kernel(h_ref, gpre_ref, wg_ref, wu_ref, wd_ref, gpost_ref, o_ref, xn_ref):
    j = pl.program_id(1)

    @pl.when(j == 0)
    def _():
        _rms_to_bf16(h_ref, gpre_ref, xn_ref)
        o_ref[...] = jnp.zeros_like(o_ref)

    xn = xn_ref[...]
    g = jnp.dot(xn, wg_ref[...], preferred_element_type=F32)
    u = jnp.dot(xn, wu_ref[...], preferred_element_type=F32)
    a = (_silu(g) * u).astype(BF16)
    d_model = o_ref.shape[1]
    cc = min(COL_CHUNK, d_model)
    for n in range(d_model // cc):
        sl = slice(n * cc, (n + 1) * cc)
        o_ref[:, sl] += jnp.dot(a, wd_ref[:, sl], preferred_element_type=F32)

    @pl.when(j == pl.num_programs(1) - 1)
    def _():
        def chunk(rows):
            _add_scaled_rows(h_ref, gpost_ref, o_ref, rows, 0.5 * _row_inv_rms(o_ref, rows))

        _for_row_chunks(o_ref.shape[0], chunk)


def _ffn(h, g_pre, wg, wu, wd, g_post, *, tm, tf):
    T, D = h.shape
    F = wg.shape[1]
    return pl.pallas_call(
        _ffn_kernel,
        grid=(T // tm, F // tf),
        in_specs=[
            pl.BlockSpec((tm, D), lambda i, j: (i, 0)),
            pl.BlockSpec((1, D), lambda i, j: (0, 0)),
            pl.BlockSpec((D, tf), lambda i, j: (0, j)),
            pl.BlockSpec((D, tf), lambda i, j: (0, j)),
            pl.BlockSpec((tf, D), lambda i, j: (j, 0)),
            pl.BlockSpec((1, D), lambda i, j: (0, 0)),
        ],
        out_specs=pl.BlockSpec((tm, D), lambda i, j: (i, 0)),
        out_shape=jax.ShapeDtypeStruct((T, D), F32),
        scratch_shapes=[pltpu.VMEM((tm, D), BF16)],
        compiler_params=_cparams(2),
        name="ffn",
    )(h, g_pre, wg, wu, wd, g_post)


RIDER_BLOCK_BYTES = 4 * 1024 * 1024


def _rider_rows(n_rows, n_cols, n_steps):
    for r in range(BF16_SUBLANES, n_rows + 1, BF16_SUBLANES):
        if n_rows % r == 0 and n_rows // r <= n_steps:
            return r if r * n_cols * 4 <= RIDER_BLOCK_BYTES else None
    return None


def _rider_specs(w_stack, layer, step_of, n_steps):
    _, n_rows, n_cols = w_stack.shape
    r = _rider_rows(n_rows, n_cols, n_steps)
    nb = n_rows // r
    src = pl.BlockSpec((None, r, n_cols), lambda *ids: (layer, (step_of(*ids) * nb) // n_steps, 0))
    dst = pl.BlockSpec((r, n_cols), lambda *ids: ((step_of(*ids) * nb) // n_steps, 0))
    return src, dst, jax.ShapeDtypeStruct((n_rows, n_cols), BF16)


def _proj_kernel(h_ref, g_ref, w_ref, o_ref, of_ref, xn_ref, *, f32_lanes):
    @pl.when(pl.program_id(1) == 0)
    def _():
        _rms_to_bf16(h_ref, g_ref, xn_ref)

    z = jnp.dot(xn_ref[...], w_ref[...], preferred_element_type=F32)
    o_ref[...] = z.astype(o_ref.dtype)

    @pl.when(pl.program_id(1) == pl.num_programs(1) - 1)
    def _():
        of_ref[...] = z[:, f32_lanes:f32_lanes + LANES]


def _proj(h, g, w, *, tm, tn, f32_col):
    T, D = h.shape
    N = w.shape[1]
    assert f32_col // tn == N // tn - 1 and f32_col % LANES == 0
    return pl.pallas_call(
        functools.partial(_proj_kernel, f32_lanes=f32_col % tn),
        grid=(T // tm, N // tn),
        in_specs=[
            pl.BlockSpec((tm, D), lambda i, j: (i, 0)),
            pl.BlockSpec((1, D), lambda i, j: (0, 0)),
            pl.BlockSpec((D, tn), lambda i, j: (0, j)),
        ],
        out_specs=[
            pl.BlockSpec((tm, tn), lambda i, j: (i, j)),
            pl.BlockSpec((tm, LANES), lambda i, j: (i, 0)),
        ],
        out_shape=[jax.ShapeDtypeStruct((T, N), BF16), jax.ShapeDtypeStruct((T, LANES), F32)],
        scratch_shapes=[pltpu.VMEM((tm, D), BF16)],
        compiler_params=_cparams(2),
        name="mix_in_proj",
    )(h, g, w)


def _rope_kernel(pos_ref, fr_ref, fm_ref, sr_ref, mc_ref, ms_ref, cr_ref, snr_ref, cm_ref, snm_ref):
    pos = pos_ref[...]
    ang_r = pos * fr_ref[...]
    cr_ref[...] = jnp.cos(ang_r)
    snr_ref[...] = jnp.sin(ang_r) * sr_ref[...]
    ang_m = pos * fm_ref[...]
    cm_ref[...] = jnp.cos(ang_m) * mc_ref[...]
    snm_ref[...] = jnp.sin(ang_m) * ms_ref[...]


def _rope_tables(pos_col, *, tm):
    T = pos_col.shape[0]
    lane = np.arange(LANES)
    half_r = HEAD_DIM // 2
    half_m = MLA_ROPE // 2
    freq_r = (ROPE_THETA ** (-(lane % half_r).astype(np.float32) / half_r)).astype(np.float32)
    sign_r = np.where(lane < half_r, -1.0, 1.0).astype(np.float32)
    active = (lane % 64) < half_m
    freq_m = (ROPE_THETA ** (-(lane % 64 % half_m).astype(np.float32) / half_m)).astype(np.float32)
    mask_c = active.astype(np.float32)
    mask_s = np.where(active, np.where(lane < 64, -1.0, 1.0), 0.0).astype(np.float32)
    consts = [jnp.asarray(c[None, :]) for c in (freq_r, freq_m, sign_r, mask_c, mask_s)]
    row = pl.BlockSpec((1, LANES), lambda i: (0, 0))
    tab = pl.BlockSpec((tm, LANES), lambda i: (i, 0))
    return pl.pallas_call(
        _rope_kernel,
        grid=(T // tm,),
        in_specs=[pl.BlockSpec((tm, 1), lambda i: (i, 0)), row, row, row, row, row],
        out_specs=[tab, tab, tab, tab],
        out_shape=[jax.ShapeDtypeStruct((T, LANES), F32)] * 4,
        compiler_params=_cparams(1),
        name="rope_tables",
    )(pos_col, *consts)


def _rot(x, cos, sin):
    return x * cos + pltpu.roll(x, LANES // 2, 1) * sin


def _ret_kernel(q_ref, k_ref, v_ref, g_ref, cos_ref, sin_ref, dm_ref, qd_ref, kd_ref, cd_ref,
                gain_ref, bias_ref, o_ref, st_ref, *, n_heads):
    @pl.when(pl.program_id(1) == 0)
    def _():
        st_ref[...] = jnp.zeros_like(st_ref)

    cos = cos_ref[...]
    sin = sin_ref[...]
    for h in range(n_heads):
        sl = slice(h * HEAD_DIM, (h + 1) * HEAD_DIM)
        qr = _rot(q_ref[:, sl].astype(F32), cos, sin)
        kr = _rot(k_ref[:, sl].astype(F32), cos, sin)
        vb = v_ref[:, sl]
        sc = lax.dot_general(qr.astype(BF16), kr.astype(BF16), (((1,), (1,)), ((), ())),
                             preferred_element_type=F32) * dm_ref[h]
        inner = jnp.dot(sc.astype(BF16), vb, preferred_element_type=F32)
        st = st_ref[h]
        cross = jnp.dot((qr * qd_ref[h]).astype(BF16), st.astype(BF16), preferred_element_type=F32)
        kdt = (kr * kd_ref[h]).T.astype(BF16)
        st_ref[h] = st * cd_ref[h] + jnp.dot(kdt, vb, preferred_element_type=F32)
        o = inner + cross
        mu = jnp.mean(o, axis=-1, keepdims=True)
        oc = o - mu
        var = jnp.mean(oc * oc, axis=-1, keepdims=True)
        y = (oc * lax.rsqrt(var + LN_EPS)) * gain_ref[:, sl] + bias_ref[:, sl]
        o_ref[:, sl] = (y * _silu(g_ref[:, sl].astype(F32))).astype(o_ref.dtype)


def _retention(proj, cos_r, sin_r, gain, bias, *, B, S, GW, C):
    T = B * S
    H = GW // HEAD_DIM
    nC = S // C
    idx = np.arange(C, dtype=np.float64)
    log_gamma = np.log(1.0 - 2.0 ** (-5.0 - np.arange(H, dtype=np.float64)))[:, None, None]
    diff = idx[:, None] - idx[None, :]
    dmat = np.where(diff >= 0, np.exp(log_gamma * np.maximum(diff, 0.0)), 0.0)
    qdec = np.broadcast_to(np.exp(log_gamma * (idx + 1.0)[None, :, None]), (H, C, LANES))
    kdec = np.broadcast_to(np.exp(log_gamma * (C - 1.0 - idx)[None, :, None]), (H, C, LANES))
    cdec = np.broadcast_to(np.exp(log_gamma * C), (H, 1, LANES))
    consts = [jnp.asarray(np.ascontiguousarray(c), F32) for c in (dmat, qdec, kdec, cdec)]

    def col(c):
        return pl.BlockSpec((C, GW), lambda b, i, c=c: (b * nC + i, c))

    def const(shape):
        return pl.BlockSpec(shape, lambda b, i: (0,) * len(shape))

    tab = pl.BlockSpec((C, LANES), lambda b, i: (b * nC + i, 0))
    return pl.pallas_call(
        functools.partial(_ret_kernel, n_heads=H),
        grid=(B, nC),
        in_specs=[col(0), col(1), col(2), col(3), tab, tab,
                  const((H, C, C)), const((H, C, LANES)), const((H, C, LANES)), const((H, 1, LANES)),
                  const((1, GW)), const((1, GW))],
        out_specs=pl.BlockSpec((C, GW), lambda b, i: (b * nC + i, 0)),
        out_shape=jax.ShapeDtypeStruct((T, GW), BF16),
        scratch_shapes=[pltpu.VMEM((H, HEAD_DIM, HEAD_DIM), F32)],
        compiler_params=_cparams(2),
        name="retention",
    )(proj, proj, proj, proj, cos_r, sin_r, *consts, gain, bias)


def _mla_prep_kernel(cq_ref, ckv_ref, kr_ref, gq_ref, gkv_ref, wq_ref, wkv_ref, cos_ref, sin_ref,
                     qn_ref, qr_ref, kn_ref, v_ref, kro_ref, *, gw):
    cos = cos_ref[...]
    sin = sin_ref[...]
    cqn = _rms(cq_ref[...].astype(F32), gq_ref[...]).astype(BF16)
    q = jnp.dot(cqn, wq_ref[...], preferred_element_type=F32)
    qn_ref[...] = q[:, :gw].astype(BF16)
    for h in range(gw // HEAD_DIM):
        sl = slice(h * HEAD_DIM, (h + 1) * HEAD_DIM)
        qr_ref[:, sl] = _rot(q[:, gw + h * HEAD_DIM:gw + (h + 1) * HEAD_DIM], cos, sin).astype(BF16)
    ckn = _rms(ckv_ref[...].astype(F32), gkv_ref[...]).astype(BF16)
    kv = jnp.dot(ckn, wkv_ref[...], preferred_element_type=F32)
    kn_ref[...] = kv[:, :gw].astype(BF16)
    v_ref[...] = kv[:, gw:].astype(BF16)
    kro_ref[...] = _rot(kr_ref[...].astype(F32), cos, sin).astype(BF16)


def _mla_prep(proj, gq, gkv, wq, wkv, cos_m, sin_m, *, GW, KVL, tm, off_cq, off_ckv, off_kr):
    T = proj.shape[0]
    row = lambda n: pl.BlockSpec((1, n), lambda i: (0, 0))
    blk = lambda n: pl.BlockSpec((tm, n), lambda i: (i, 0))
    return pl.pallas_call(
        functools.partial(_mla_prep_kernel, gw=GW),
        grid=(T // tm,),
        in_specs=[
            pl.BlockSpec((tm, GW), lambda i: (i, off_cq // GW)),
            pl.BlockSpec((tm, KVL), lambda i: (i, off_ckv // KVL)),
            pl.BlockSpec((tm, LANES), lambda i: (i, off_kr // LANES)),
            row(GW), row(KVL),
            pl.BlockSpec((GW, 2 * GW), lambda i: (0, 0)),
            pl.BlockSpec((KVL, 2 * GW), lambda i: (0, 0)),
            blk(LANES), blk(LANES),
        ],
        out_specs=[blk(GW), blk(GW), blk(GW), blk(GW), blk(LANES)],
        out_shape=[jax.ShapeDtypeStruct((T, GW), BF16)] * 4 + [jax.ShapeDtypeStruct((T, LANES), BF16)],
        compiler_params=_cparams(1),
        name="mla_prep",
    )(proj, proj, proj, gq, gkv, wq, wkv, cos_m, sin_m)


def _split3(x):
    hi = x.astype(BF16)
    r1 = x - hi.astype(F32)
    mid = r1.astype(BF16)
    lo = (r1 - mid.astype(F32)).astype(BF16)
    return hi, mid, lo


def _fox_prep_kernel(lg_ref, fb_ref, tri_ref, selq_ref, selk_ref, oneq_ref, onek_ref,
                     qe_ref, ke_ref, carry_ref, *, tiles_per_seq):
    @pl.when(pl.program_id(0) % tiles_per_seq == 0)
    def _():
        carry_ref[...] = jnp.zeros_like(carry_ref)

    x = lg_ref[...] + fb_ref[...]
    logf = jnp.minimum(x, 0.0) - jnp.log(1.0 + jnp.exp(-jnp.abs(x)))
    tri = tri_ref[...]
    cum = carry_ref[...]
    for part in _split3(logf):
        cum = cum + jnp.dot(tri, part, preferred_element_type=F32)
    carry_ref[...] = cum[cum.shape[0] - 1:, :]
    parts = jnp.concatenate(_split3(cum * LOG2E), axis=-1)
    qe_ref[...] = (jnp.dot(parts, selq_ref[...], preferred_element_type=F32) + oneq_ref[...]).astype(BF16)
    ke_ref[...] = (jnp.dot(parts, selk_ref[...], preferred_element_type=F32) + onek_ref[...]).astype(BF16)


def _fox_prep(logits, fbias, *, S, GW, tm):
    T = logits.shape[0]
    H = GW // HEAD_DIM
    tri = np.tril(np.ones((tm, tm), np.float32))
    selq = np.zeros((3 * LANES, GW), np.float32)
    selk = np.zeros((3 * LANES, GW), np.float32)
    oneq = np.zeros((1, GW), np.float32)
    onek = np.zeros((1, GW), np.float32)
    for h in range(H):
        for part in range(3):
            selq[part * LANES + h, h * HEAD_DIM + part] = 1.0
            selk[part * LANES + h, h * HEAD_DIM + 3 + part] = -1.0
            oneq[0, h * HEAD_DIM + 3 + part] = 1.0
            onek[0, h * HEAD_DIM + part] = 1.0
    const = lambda a: pl.BlockSpec(a.shape, lambda i: (0, 0))
    consts = [jnp.asarray(tri, BF16), jnp.asarray(selq, BF16), jnp.asarray(selk, BF16),
              jnp.asarray(oneq), jnp.asarray(onek)]
    return pl.pallas_call(
        functools.partial(_fox_prep_kernel, tiles_per_seq=S // tm),
        grid=(T // tm,),
        in_specs=[pl.BlockSpec((tm, LANES), lambda i: (i, 0)), pl.BlockSpec((1, LANES), lambda i: (0, 0))]
                 + [const(c) for c in consts],
        out_specs=[pl.BlockSpec((tm, GW), lambda i: (i, 0))] * 2,
        out_shape=[jax.ShapeDtypeStruct((T, GW), BF16)] * 2,
        scratch_shapes=[pltpu.VMEM((1, LANES), F32)],
        compiler_params=_cparams(1),
        name="fox_prep",
    )(logits, fbias, *consts)


ATTN_ROWS = 512
LOG2E = 1.4426950408889634


def _lane_tile(x, n):
    return x if n == 1 else jnp.concatenate([x] * n, axis=1)


def _flash_kernel(qm_ref, qe_ref, km_ref, ke_ref, v_ref, *rest, tq, tk, n_riders):
    rider_src, (o_ref, *rider_dst), (m_ref, acc_ref) = rest[:n_riders], rest[n_riders:2 * n_riders + 1], rest[-2:]
    i = pl.program_id(2)
    rg = min(ATTN_ROWS, tq)
    m_ref[...] = jnp.full_like(m_ref, NEG_BIG)
    acc_ref[...] = jnp.zeros_like(acc_ref)

    def chain(g, k_rows, n_k, causal_tail):
        q_rows = slice(g * rg, (g + 1) * rg)
        q = jnp.concatenate([qm_ref[q_rows, :], qe_ref[q_rows, :]], axis=-1)
        k = jnp.concatenate([km_ref[k_rows, :], ke_ref[k_rows, :]], axis=-1)
        v = jnp.concatenate([v_ref[k_rows, :], jnp.ones((n_k, HEAD_DIM), BF16)], axis=-1)
        s = lax.dot_general(q, k, (((1,), (1,)), ((), ())), preferred_element_type=F32)
        if causal_tail:
            r = lax.broadcasted_iota(jnp.int32, (rg, rg), 0)
            c = lax.broadcasted_iota(jnp.int32, (rg, rg), 1)
            tail = jnp.where(c <= r, s[:, n_k - rg:], NEG_BIG)
            s = tail if n_k == rg else jnp.concatenate([s[:, :n_k - rg], tail], axis=1)
        m_prev = m_ref[q_rows, :]
        m_new = jnp.maximum(m_prev, jnp.max(s, axis=-1, keepdims=True))
        alpha = jnp.exp2(m_prev - m_new)
        p = jnp.exp2(s - _lane_tile(m_new, n_k // LANES))
        pv = jnp.dot(p.astype(BF16), v, preferred_element_type=F32)
        acc_ref[q_rows, :] = _lane_tile(alpha, 2) * acc_ref[q_rows, :] + pv
        m_ref[q_rows, :] = m_new

    blocks_per_trip = tq // tk

    def body(j, carry):
        for u in range(blocks_per_trip):
            k_rows = pl.ds(pl.multiple_of((j * blocks_per_trip + u) * tk, tk), tk)
            for g in range(tq // rg):
                chain(g, k_rows, tk, False)
        return carry

    lax.fori_loop(0, i, body, 0)
    for src, dst in zip(rider_src, rider_dst):
        dst[...] = src[...].astype(BF16)
    base = pl.multiple_of(i * tq, tq)
    for g in reversed(range(tq // rg)):
        n_k = (g + 1) * rg
        chain(g, pl.ds(base, n_k), n_k, True)
    acc = acc_ref[...]
    o_ref[...] = (acc[:, :HEAD_DIM] / acc[:, HEAD_DIM:]).astype(o_ref.dtype)


def _flash(qm, qe, km, ke, v, *, B, S, H, tq, tk, qm_off=0, km_off=0, v_off=0, ke_shared=False, cast=None):
    T = B * S
    nq = S // tq
    n_steps = B * H * nq
    qspec = lambda off: pl.BlockSpec((tq, HEAD_DIM), lambda b, h, i: (b * nq + i, off + h))
    kspec = lambda off: pl.BlockSpec((S, HEAD_DIM), lambda b, h, i: (b, off + h))
    ke_spec = pl.BlockSpec((S, HEAD_DIM), (lambda b, h, i: (b, 0)) if ke_shared else (lambda b, h, i: (b, h)))
    in_specs = [qspec(qm_off), qspec(0), kspec(km_off), ke_spec, kspec(v_off)]
    out_specs = [pl.BlockSpec((tq, HEAD_DIM), lambda b, h, i: (b * nq + i, h))]
    out_shape = [jax.ShapeDtypeStruct((T, H * HEAD_DIM), BF16)]
    stacks, layer = cast if cast is not None else ((), 0)
    for w_stack in stacks:
        src, dst, shape = _rider_specs(w_stack, layer, lambda b, h, i: (b * H + h) * nq + i, n_steps)
        in_specs.append(src)
        out_specs.append(dst)
        out_shape.append(shape)
    res = pl.pallas_call(
        functools.partial(_flash_kernel, tq=tq, tk=tk, n_riders=len(stacks)),
        grid=(B, H, nq),
        in_specs=in_specs,
        out_specs=out_specs,
        out_shape=out_shape,
        scratch_shapes=[pltpu.VMEM((tq, LANES), F32), pltpu.VMEM((tq, 2 * HEAD_DIM), F32)],
        compiler_params=_cparams(3),
        name="causal_attention",
    )(qm, qe, km, ke, v, *stacks)
    return res[0], tuple(res[1:])


def _dwconv_kernel(a_ref, b_ref, ba_ref, bb_ref, w_ref, wb_ref, o_ref, u_ref, halo_ref, *, tm, tiles_per_seq):
    c = pl.program_id(1)

    @pl.when(pl.program_id(0) % tiles_per_seq == 0)
    def _():
        halo_ref[c] = jnp.zeros((CONV_HALO, LANES), F32)

    u = (a_ref[...].astype(F32) + ba_ref[...]) * jax.nn.sigmoid(b_ref[...].astype(F32) + bb_ref[...])
    u_ref[:CONV_HALO, :] = halo_ref[c]
    u_ref[CONV_HALO:, :] = u
    halo_ref[c] = u[tm - CONV_HALO:, :]
    acc = jnp.zeros((tm, LANES), F32) + wb_ref[...]
    for j in range(CONV_K):
        start = CONV_HALO - (CONV_K - 1) + j
        acc = acc + u_ref[start:start + tm, :] * w_ref[j:j + 1, :]
    o_ref[...] = acc


def _dwconv(proj, glu_bias_a, glu_bias_b, w_dw, b_dw, *, S, GW, tm, off_a, off_b):
    T = proj.shape[0]
    nc = GW // LANES
    row = pl.BlockSpec((1, LANES), lambda i, c: (0, c))
    return pl.pallas_call(
        functools.partial(_dwconv_kernel, tm=tm, tiles_per_seq=S // tm),
        grid=(T // tm, nc),
        in_specs=[
            pl.BlockSpec((tm, LANES), lambda i, c: (i, off_a // LANES + c)),
            pl.BlockSpec((tm, LANES), lambda i, c: (i, off_b // LANES + c)),
            row, row,
            pl.BlockSpec((CONV_HALO, LANES), lambda i, c: (0, c)),
            row,
        ],
        out_specs=pl.BlockSpec((tm, LANES), lambda i, c: (i, c)),
        out_shape=jax.ShapeDtypeStruct((T, GW), F32),
        scratch_shapes=[pltpu.VMEM((tm + CONV_HALO, LANES), F32), pltpu.VMEM((nc, CONV_HALO, LANES), F32)],
        compiler_params=_cparams(2),
        name="glu_dwconv",
    )(proj, proj, glu_bias_a, glu_bias_b, w_dw, b_dw)


def _conv_pw_kernel(y_ref, g_ref, b_ref, w_ref, pb_ref, o_ref):
    y = y_ref[...]
    mu = jnp.mean(y, axis=-1, keepdims=True)
    yc = y - mu
    var = jnp.mean(yc * yc, axis=-1, keepdims=True)
    z = _silu(yc * lax.rsqrt(var + LN_EPS) * g_ref[...] + b_ref[...]).astype(BF16)
    o_ref[...] = (jnp.dot(z, w_ref[...], preferred_element_type=F32) + pb_ref[...]).astype(o_ref.dtype)


def _conv_pw(y, gain, bias, w_pw, pw_bias, *, tm):
    T, GW = y.shape
    row = pl.BlockSpec((1, GW), lambda i: (0, 0))
    blk = pl.BlockSpec((tm, GW), lambda i: (i, 0))
    return pl.pallas_call(
        _conv_pw_kernel,
        grid=(T // tm,),
        in_specs=[blk, row, row, pl.BlockSpec((GW, GW), lambda i: (0, 0)), row],
        out_specs=blk,
        out_shape=jax.ShapeDtypeStruct((T, GW), BF16),
        compiler_params=_cparams(1),
        name="conv_ln_pointwise",
    )(y, gain, bias, w_pw, pw_bias)


def _park_tile(o_ref, ss_ref, z, j, tn):
    o_ref[:, pl.ds(pl.multiple_of(j * tn, tn), tn)] = z
    ss_ref[...] += jnp.sum(z * z, axis=-1, keepdims=True)


def _finish_rows(h_ref, gain_ref, o_ref, ss_ref):
    d_model = o_ref.shape[1]

    def chunk(rows):
        r = lax.rsqrt(ss_ref[rows, :] * (1.0 / d_model) + NORM_EPS)
        _add_scaled_rows(h_ref, gain_ref, o_ref, rows, r)

    _for_row_chunks(o_ref.shape[0], chunk)


def _out_proj_kernel(y0_ref, y1_ref, y2_ref, y3_ref, w_ref, h_ref, g_ref, o_ref, ss_ref,
                     *, gw, tn, n_tiles):
    j = pl.program_id(1)

    @pl.when(j == 0)
    def _():
        ss_ref[...] = jnp.zeros_like(ss_ref)

    z = jnp.dot(y0_ref[...], w_ref[0:gw, :], preferred_element_type=F32)
    z += jnp.dot(y1_ref[...], w_ref[gw:2 * gw, :], preferred_element_type=F32)
    z += jnp.dot(y2_ref[...], w_ref[2 * gw:3 * gw, :], preferred_element_type=F32)
    z += jnp.dot(y3_ref[...], w_ref[3 * gw:4 * gw, :], preferred_element_type=F32)
    _park_tile(o_ref, ss_ref, z, j, tn)

    @pl.when(j == n_tiles - 1)
    def _():
        _finish_rows(h_ref, g_ref, o_ref, ss_ref)


def _out_proj(ys, w, h, g, *, tm, tn):
    T, D = h.shape
    GW = ys[0].shape[1]
    n_tiles = D // tn
    yspec = pl.BlockSpec((tm, GW), lambda i, j: (i, 0))
    return pl.pallas_call(
        functools.partial(_out_proj_kernel, gw=GW, tn=tn, n_tiles=n_tiles),
        grid=(T // tm, n_tiles),
        in_specs=[yspec, yspec, yspec, yspec,
                  pl.BlockSpec((N_GROUPS * GW, tn), lambda i, j: (0, j)),
                  pl.BlockSpec((tm, D), lambda i, j: (i, 0)),
                  pl.BlockSpec((1, D), lambda i, j: (0, 0))],
        out_specs=pl.BlockSpec((tm, D), lambda i, j: (i, 0)),
        out_shape=jax.ShapeDtypeStruct((T, D), F32),
        scratch_shapes=[pltpu.VMEM((tm, 1), F32)],
        compiler_params=_cparams(2),
        name="mix_out_proj",
    )(*ys, w, h, g)


def _ple_kernel(h_ref, gin_ref, wg_ref, p_ref, wp_ref, gpost_ref, o_ref, xn_ref, ss_ref, *, tn, n_tiles):
    j = pl.program_id(1)

    @pl.when(j == 0)
    def _():
        _rms_to_bf16(h_ref, gin_ref, xn_ref)
        ss_ref[...] = jnp.zeros_like(ss_ref)

    gate = jax.nn.sigmoid(jnp.dot(xn_ref[...], wg_ref[...], preferred_element_type=F32))
    z = gate * jnp.dot(p_ref[...], wp_ref[...], preferred_element_type=F32)
    _park_tile(o_ref, ss_ref, z, j, tn)

    @pl.when(j == n_tiles - 1)
    def _():
        _finish_rows(h_ref, gpost_ref, o_ref, ss_ref)


def _ple(h, g_in, wg, p, wp, g_post, *, tm, tn):
    T, D = h.shape
    P = p.shape[1]
    n_tiles = D // tn
    return pl.pallas_call(
        functools.partial(_ple_kernel, tn=tn, n_tiles=n_tiles),
        grid=(T // tm, n_tiles),
        in_specs=[pl.BlockSpec((tm, D), lambda i, j: (i, 0)),
                  pl.BlockSpec((1, D), lambda i, j: (0, 0)),
                  pl.BlockSpec((D, tn), lambda i, j: (0, j)),
                  pl.BlockSpec((tm, P), lambda i, j: (i, 0)),
                  pl.BlockSpec((P, tn), lambda i, j: (0, j)),
                  pl.BlockSpec((1, D), lambda i, j: (0, 0))],
        out_specs=pl.BlockSpec((tm, D), lambda i, j: (i, 0)),
        out_shape=jax.ShapeDtypeStruct((T, D), F32),
        scratch_shapes=[pltpu.VMEM((tm, D), BF16), pltpu.VMEM((tm, 1), F32)],
        compiler_params=_cparams(2),
        name="ple",
    )(h, g_in, wg, p, wp, g_post)


def _rope_lane_layout(w, half):
    z = jnp.zeros((w.shape[0], LANES // 2 - half), w.dtype)
    return jnp.concatenate([w[:, :half], z, w[:, half:], z], axis=1)


def _plan(D, F, S, T):
    return dict(
        tm=_tile(T if T < S else S, 512),
        tf=_tile(F, 256),
        tn=_tile(D, 1024),
        tn_proj=1024,
        tq=_tile(S, 2048),
        tk=_tile(S, 512),
        c_ret=_tile(S, 256),
    )


def kernel(x, p, positions, ffn1_norm_pre, ffn1_w_gate, ffn1_w_up, ffn1_w_down, ffn1_norm_post, mix_norm_pre, w_in, mla_q_norm, mla_w_uq, mla_kv_norm, mla_w_ukv, ret_gn_gain, ret_gn_bias, fox_forget_bias, conv_glu_bias, conv_dw, conv_dw_bias, conv_ln_gain, conv_ln_bias, conv_w_pw, conv_pw_bias, w_out, mix_norm_post, ffn2_norm_pre, ffn2_w_gate, ffn2_w_up, ffn2_w_down, ffn2_norm_post, ple_norm_in, ple_w_gate, ple_w_proj, ple_norm_post):
    B, S, D = x.shape
    depth = p.shape[0]
    T = B * S
    F = ffn1_w_gate.shape[2]
    GW = D // N_GROUPS
    H = GW // HEAD_DIM
    KVL = mla_kv_norm.shape[1]
    NOPE = HEAD_DIM
    plan = _plan(D, F, S, T)
    tm, tf, tn, tq, tk, c_ret = (plan[n] for n in ("tm", "tf", "tn", "tq", "tk", "c_ret"))
    row = lambda v: v.reshape(1, -1).astype(F32)

    h = x.reshape(T, D)
    pos_col = positions.reshape(T, 1).astype(F32)
    cos_r, sin_r, cos_m, sin_m = _rope_tables(pos_col, tm=tm)

    sp = np.cumsum([GW] * 4 + [GW, KVL, MLA_ROPE] + [GW] * 3 + [H, 2 * GW])
    off_cq, off_ckv, off_kr = 4 * GW, 5 * GW, 5 * GW + KVL
    off_fox = off_kr + LANES
    off_conv = off_fox + 3 * GW
    n_main = off_conv + 2 * GW
    tn_proj = plan["tn_proj"]
    n_main_pad = -(-(n_main + LANES) // tn_proj) * tn_proj
    assert mla_q_norm.shape[1] == GW and off_ckv % KVL == 0 and KVL % LANES == 0
    mla_scale = (NOPE + MLA_ROPE) ** -0.5 * LOG2E
    col_scale = np.ones((int(sp[-1]),), np.float32)
    col_scale[sp[0]:sp[1]] = HEAD_DIM ** -0.5
    col_scale[sp[6]:sp[7]] = HEAD_DIM ** -0.5 * LOG2E

    ffn1_stacks = (ffn1_w_gate, ffn1_w_up, ffn1_w_down)
    ffn2_stacks = (ffn2_w_gate, ffn2_w_up, ffn2_w_down)
    n_attn_steps = B * H * (S // tq)
    ride = all(_rider_rows(w_.shape[1], w_.shape[2], n_attn_steps) is not None for w_ in ffn1_stacks)

    def ffn_weights(stacks, layer, ridden):
        return ridden if ride else tuple(_to_bf16(w_, layer) for w_ in stacks)

    ffn1_w = tuple(_to_bf16(w_, 0) for w_ in ffn1_stacks)

    for i in range(depth):
        sp_t = tuple(int(v) for v in sp)
        if _can_read_w_in_transposed(sp_t, off_kr, n_main, int(sp[-1])):
            w_main = _mix_in_weights_t(jnp.swapaxes(w_in, 1, 2), col_scale, i, sp=sp_t,
                                       off_kr=off_kr, n_main=n_main, n_pad=n_main_pad)
        else:
            w_main = _mix_in_weights(w_in, jnp.asarray(col_scale[None, :]), i, sp=sp_t,
                                     off_kr=off_kr, n_main=n_main, n_pad=n_main_pad)
        uq = mla_w_uq[i].reshape(GW, H, NOPE + MLA_ROPE) * mla_scale
        uq_nope = uq[:, :, :NOPE].reshape(GW, H * NOPE)
        uq_rope = jnp.concatenate(
            [_rope_lane_layout(uq[:, hh, NOPE:], MLA_ROPE // 2) for hh in range(H)], axis=1)
        w_uq = jnp.concatenate([uq_nope, uq_rope], axis=1).astype(BF16)
        ukv = mla_w_ukv[i].reshape(KVL, H, 2 * HEAD_DIM)
        w_ukv = jnp.concatenate([ukv[:, :, :NOPE].reshape(KVL, GW), ukv[:, :, NOPE:].reshape(KVL, GW)],
                                axis=1).astype(BF16)
        dw = jnp.pad(conv_dw[i], ((0, CONV_HALO - CONV_K), (0, 0)))

        h = _ffn(h, row(ffn1_norm_pre[i]), *ffn1_w, row(ffn1_norm_post[i]), tm=tm, tf=tf)

        proj, logits = _proj(h, row(mix_norm_pre[i]), w_main, tm=tm, tn=tn_proj, f32_col=n_main)
        y_ret = _retention(proj, cos_r, sin_r, row(ret_gn_gain[i]), row(ret_gn_bias[i]), B=B, S=S, GW=GW, C=c_ret)
        qn, qr, kn, mv, kro = _mla_prep(proj, row(mla_q_norm[i]), row(mla_kv_norm[i]), w_uq, w_ukv, cos_m, sin_m,
                                        GW=GW, KVL=KVL, tm=tm, off_cq=off_cq, off_ckv=off_ckv, off_kr=off_kr)
        y_mla, ridden = _flash(qn, qr, kn, kro, mv, B=B, S=S, H=H, tq=tq, tk=tk, ke_shared=True,
                               cast=(ffn2_stacks, i) if ride else None)
        ffn2_w = ffn_weights(ffn2_stacks, i, ridden)
        fbias = jnp.pad(fox_forget_bias[i], (0, LANES - H)).reshape(1, LANES)
        qe, ke = _fox_prep(logits, fbias, S=S, GW=GW, tm=tm)
        hb = off_fox // HEAD_DIM
        more = i + 1 < depth
        y_fox, ridden = _flash(proj, qe, proj, ke, proj, B=B, S=S, H=H, tq=tq, tk=tk,
                               qm_off=hb, km_off=hb + H, v_off=hb + 2 * H,
                               cast=(ffn1_stacks, i + 1) if ride and more else None)
        if more:
            ffn1_w = ffn_weights(ffn1_stacks, i + 1, ridden)
        gb = conv_glu_bias[i]
        y_dw = _dwconv(proj, row(gb[:GW]), row(gb[GW:]), dw, row(conv_dw_bias[i]),
                       S=S, GW=GW, tm=tm, off_a=off_conv, off_b=off_conv + GW)
        y_conv = _conv_pw(y_dw, row(conv_ln_gain[i]), row(conv_ln_bias[i]), conv_w_pw[i].astype(BF16),
                          row(conv_pw_bias[i]), tm=tm)
        h = _out_proj([y_ret, y_mla, y_fox, y_conv], _to_bf16(w_out, i), h, row(mix_norm_post[i]), tm=tm, tn=tn)

        h = _ffn(h, row(ffn2_norm_pre[i]), *ffn2_w, row(ffn2_norm_post[i]), tm=tm, tf=tf)

        h = _ple(h, row(ple_norm_in[i]), _to_bf16(ple_w_gate, i), p[i].reshape(T, -1).astype(BF16),
                 ple_w_proj[i].astype(BF16), row(ple_norm_post[i]), tm=tm, tn=tn)

    return h.reshape(B, S, D)
```
